```python
import math
import jax, jax.numpy as jnp
from jax import lax
import numpy as np

D_MODEL = 1024
BATCH = 8
SEQ = 4096
DEPTH = 4

N_MIXERS = 2
ATTN_HEADS = 8
ATTN_HEAD_DIM = D_MODEL // (2 * ATTN_HEADS)
ATTN_V_DIM = 2 * ATTN_HEAD_DIM
ATTN_WIDTH = ATTN_HEADS * ATTN_V_DIM
ATTN_IN = 4 * ATTN_WIDTH
Q_BLOCK = 128
LAMBDA_STD = 0.1
SGU_EXPAND = 2
SGU_WIDTH = SGU_EXPAND * D_MODEL
SGU_GROUPS = 8
SGU_GROUP_DIM = SGU_WIDTH // SGU_GROUPS
SGU_IN = 3 * SGU_WIDTH
CHUNK = 128
PLE_DIM = 256

N_ATTN_LAYERS = (DEPTH + 1) // 2
N_SGU_LAYERS = DEPTH // 2
NORM_EPS = 1e-6
SUBLN_EPS = 1e-5
LN_EPS = 1e-5

kernel_name = "hybrid_diffattn_chunked_sgu_ple"


def rms_norm(x, g, eps=NORM_EPS):
    xf = x.astype(jnp.float32)
    y = xf * lax.rsqrt(jnp.mean(xf * xf, axis=-1, keepdims=True) + eps)
    return (y * g.astype(jnp.float32)).astype(x.dtype)


def layer_norm(x, g, b, eps=LN_EPS):
    xf = x.astype(jnp.float32)
    mu = jnp.mean(xf, axis=-1, keepdims=True)
    xc = xf - mu
    var = jnp.mean(xc * xc, axis=-1, keepdims=True)
    y = xc * lax.rsqrt(var + eps) * g.astype(jnp.float32) + b.astype(jnp.float32)
    return y.astype(x.dtype)


def alibi_slopes(n_heads):
    return jnp.asarray(2.0 ** (-8.0 * np.arange(1, n_heads + 1) / n_heads), dtype=jnp.float32)


def diff_attention_mixer(h, w_in, lam_q1, lam_k1, lam_q2, lam_k2, subln_g, w_out, lambda_init):
    B, S, _ = h.shape
    H, dh, dv = ATTN_HEADS, ATTN_HEAD_DIM, ATTN_V_DIM
    z = h @ w_in
    q, k, v, gate = jnp.split(z, 4, axis=-1)
    q = q.reshape(B, S, H, 2, dh)
    k = k.reshape(B, S, H, 2, dh)
    q1, q2 = q[..., 0, :], q[..., 1, :]
    k1, k2 = k[..., 0, :], k[..., 1, :]
    v = v.reshape(B, S, H, dv)
    f32 = jnp.float32
    lam = (jnp.exp(jnp.sum(lam_q1.astype(f32) * lam_k1.astype(f32)))
           - jnp.exp(jnp.sum(lam_q2.astype(f32) * lam_k2.astype(f32))) + lambda_init)
    slopes = alibi_slopes(H)
    scale = dh ** -0.5
    key_pos = jnp.arange(S)
    n_blocks = S // Q_BLOCK

    def block(i):
        start = i * Q_BLOCK
        qb1 = lax.dynamic_slice_in_dim(q1, start, Q_BLOCK, axis=1)
        qb2 = lax.dynamic_slice_in_dim(q2, start, Q_BLOCK, axis=1)
        qpos = start + jnp.arange(Q_BLOCK)
        dist = (qpos[:, None] - key_pos[None, :]).astype(f32)
        causal = dist >= 0
        bias = -slopes[:, None, None] * dist[None]

        def probs(qb, kk):
            s = jnp.einsum('bqhd,bkhd->bhqk', qb, kk).astype(f32) * scale + bias
            s = jnp.where(causal, s, -jnp.inf)
            return jax.nn.softmax(s, axis=-1)

        a = probs(qb1, k1) - lam * probs(qb2, k2)
        return jnp.einsum('bhqk,bkhd->bqhd', a.astype(v.dtype), v)

    o = lax.map(block, jnp.arange(n_blocks))
    o = jnp.moveaxis(o, 0, 1).reshape(B, S, H, dv)
    o = rms_norm(o, subln_g, SUBLN_EPS) * (1.0 - lambda_init)
    o = o.reshape(B, S, ATTN_WIDTH) * jax.nn.silu(gate)
    return o @ w_out


def sgu_mixer(h, w_in, ln_g, ln_b, w_s, b_s, w_out):
    B, S, _ = h.shape
    z = h @ w_in
    uv = jax.nn.gelu(z[..., :2 * SGU_WIDTH], approximate=False)
    gate = z[..., 2 * SGU_WIDTH:]
    u, v = jnp.split(uv, 2, axis=-1)
    v = layer_norm(v, ln_g, ln_b)
    v = v.reshape(B, S // CHUNK, CHUNK, SGU_GROUPS, SGU_GROUP_DIM)
    causal = jnp.tril(jnp.ones((CHUNK, CHUNK), dtype=w_s.dtype))
    mixed = (jnp.einsum('gts,bcsgd->bctgd', w_s * causal, v)
             + b_s.T[:, :, None])
    y = u * mixed.reshape(B, S, SGU_WIDTH) * jax.nn.silu(gate)
    return y @ w_out


def setup_inputs(seed: int = 0) -> dict:
    key = jax.random.key(seed)
    ks = jax.random.split(key, 24)
    nrm = jax.random.normal
    D = D_MODEL
    na, ns = N_ATTN_LAYERS, N_SGU_LAYERS
    return {
        "x": nrm(ks[0], (BATCH, SEQ, D), jnp.float32),
        "p": nrm(ks[1], (DEPTH, BATCH, SEQ, PLE_DIM), jnp.float32),
        "attn_norm": 1.0 + 0.02 * nrm(ks[2], (na, D), jnp.float32),
        "attn_w_in": nrm(ks[3], (na, D, ATTN_IN), jnp.float32) * D ** -0.5,
        "attn_lam_q1": LAMBDA_STD * nrm(ks[4], (na, ATTN_HEAD_DIM), jnp.float32),
        "attn_lam_k1": LAMBDA_STD * nrm(ks[5], (na, ATTN_HEAD_DIM), jnp.float32),
        "attn_lam_q2": LAMBDA_STD * nrm(ks[6], (na, ATTN_HEAD_DIM), jnp.float32),
        "attn_lam_k2": LAMBDA_STD * nrm(ks[7], (na, ATTN_HEAD_DIM), jnp.float32),
        "attn_subln": 1.0 + 0.02 * nrm(ks[8], (na, ATTN_V_DIM), jnp.float32),
        "attn_w_out": nrm(ks[9], (na, ATTN_WIDTH, D), jnp.float32) * ATTN_WIDTH ** -0.5,
        "sgu_norm": 1.0 + 0.02 * nrm(ks[10], (ns, D), jnp.float32),
        "sgu_w_in": nrm(ks[11], (ns, D, SGU_IN), jnp.float32) * D ** -0.5,
        "sgu_ln_g": 1.0 + 0.02 * nrm(ks[12], (ns, SGU_WIDTH), jnp.float32),
        "sgu_ln_b": 0.02 * nrm(ks[13], (ns, SGU_WIDTH), jnp.float32),
        "sgu_w_s": nrm(ks[14], (ns, SGU_GROUPS, CHUNK, CHUNK), jnp.float32) * CHUNK ** -0.5,
        "sgu_b_s": 1.0 + 0.02 * nrm(ks[15], (ns, SGU_GROUPS, CHUNK), jnp.float32),
        "sgu_w_out": nrm(ks[16], (ns, SGU_WIDTH, D), jnp.float32) * SGU_WIDTH ** -0.5,
        "ple_proj": nrm(ks[17], (DEPTH, PLE_DIM, D), jnp.float32) * PLE_DIM ** -0.5,
        "ple_gate": nrm(ks[18], (DEPTH, D, D), jnp.float32) * D ** -0.5,
        "final_norm": 1.0 + 0.02 * nrm(ks[19], (D,), jnp.float32),
    }


def reference(x, p, attn_norm, attn_w_in, attn_lam_q1, attn_lam_k1, attn_lam_q2, attn_lam_k2,
              attn_subln, attn_w_out, sgu_norm, sgu_w_in, sgu_ln_g, sgu_ln_b, sgu_w_s, sgu_b_s,
              sgu_w_out, ple_proj, ple_gate, final_norm):
    for i in range(DEPTH):
        j = i // N_MIXERS
        if i % N_MIXERS == 0:
            lambda_init = 0.8 - 0.6 * math.exp(-0.3 * i)
            h = rms_norm(x, attn_norm[j])
            x = x + diff_attention_mixer(h, attn_w_in[j], attn_lam_q1[j], attn_lam_k1[j],
                                         attn_lam_q2[j], attn_lam_k2[j], attn_subln[j],
                                         attn_w_out[j], lambda_init)
        else:
            h = rms_norm(x, sgu_norm[j])
            x = x + sgu_mixer(h, sgu_w_in[j], sgu_ln_g[j], sgu_ln_b[j], sgu_w_s[j],
                              sgu_b_s[j], sgu_w_out[j])
        x = x + (p[i] @ ple_proj[i]) * jax.nn.sigmoid(x @ ple_gate[i])
    return rms_norm(x, final_norm)
```

```python
import functools
import math

import jax
import jax.numpy as jnp
from jax import lax
from jax.experimental import pallas as pl
from jax.experimental.pallas import tpu as pltpu

D_MODEL = 1024
DEPTH = 4
ATTN_HEADS = 8
ATTN_HEAD_DIM = 64
ATTN_V_DIM = 128
ATTN_WIDTH = 1024
SGU_WIDTH = 2048
SGU_GROUPS = 8
SGU_GROUP_DIM = 256
CHUNK = 128
PLE_DIM = 256
NORM_EPS = 1e-6
SUBLN_EPS = 1e-5
LN_EPS = 1e-5

F32 = jnp.float32
BF16 = jnp.bfloat16

ROW_BLOCK = 512
Q_TILE = 512
K_TILE = 512
VMEM_LIMIT = 56 * 1024 * 1024


def _rms(x, g, eps):
    return x * lax.rsqrt(jnp.mean(x * x, axis=-1, keepdims=True) + eps) * g


def _gelu(x):
    return 0.5 * x * (1.0 + lax.erf(x * (1.0 / math.sqrt(2.0))))


def _silu(x):
    return x * jax.nn.sigmoid(x)


def _resident(shape):
    zeros = (0,) * len(shape)
    return pl.BlockSpec(shape, lambda *_: zeros, pipeline_mode=pl.Buffered(1))


def _params(n_axes):
    return pltpu.CompilerParams(
        dimension_semantics=("arbitrary",) * n_axes,
        vmem_limit_bytes=VMEM_LIMIT)


def _attn_in_kernel(x_ref, g_ref, w_ref, q_ref, k_ref, v_ref, gate_ref):
    h = _rms(x_ref[...], g_ref[...], NORM_EPS).astype(BF16)
    outs = (q_ref, k_ref, v_ref, gate_ref)
    for idx, o_ref in enumerate(outs):
        z = jnp.dot(h, w_ref[:, idx * ATTN_WIDTH:(idx + 1) * ATTN_WIDTH],
                    preferred_element_type=F32)
        if idx == 0:
            z = z * (ATTN_HEAD_DIM ** -0.5)
        o_ref[...] = z.astype(BF16)


def _attn_in(x, g, w):
    m = x.shape[0]
    row = pl.BlockSpec((ROW_BLOCK, D_MODEL), lambda i: (i, 0))
    out = jax.ShapeDtypeStruct((m, ATTN_WIDTH), BF16)
    return pl.pallas_call(
        _attn_in_kernel,
        grid=(m // ROW_BLOCK,),
        in_specs=[row, _resident((1, D_MODEL)), _resident((D_MODEL, 4 * ATTN_WIDTH))],
        out_specs=[row] * 4,
        out_shape=[out] * 4,
        compiler_params=_params(1),
        name="attn_in",
    )(x, g, w)


def _attn_kernel(slopes_ref, lam_ref, subln_ref, q_ref, k_ref, v_ref, gate_ref,
                 o_ref, vaug_ref, acc1_ref, acc2_ref, m1_ref, m2_ref,
                 *, lambda_init):
    hd = pl.program_id(1)
    qi = pl.program_id(2)
    tq, tk = Q_TILE, K_TILE
    slope = slopes_ref[hd]

    @pl.when(qi == 0)
    def _():
        vaug_ref[:, :ATTN_V_DIM] = v_ref[...]
        vaug_ref[:, ATTN_V_DIM:] = jnp.ones((v_ref.shape[0], ATTN_V_DIM), BF16)

    q = q_ref[...]
    lane = lax.broadcasted_iota(jnp.int32, q.shape, 1)
    zero = jnp.zeros_like(q)
    qs = (jnp.where(lane < ATTN_HEAD_DIM, q, zero),
          jnp.where(lane >= ATTN_HEAD_DIM, q, zero))
    accs = (acc1_ref, acc2_ref)
    ms = (m1_ref, m2_ref)
    col = lax.broadcasted_iota(jnp.int32, (1, tk), 1).astype(F32)

    def scores(qh, j):
        k = k_ref[pl.ds(pl.multiple_of(j * tk, tk), tk), :]
        s = lax.dot_general(qh, k, (((1,), (1,)), ((), ())),
                            preferred_element_type=F32)
        return s + slope * (col + (j * tk).astype(F32))

    def vblock(j):
        return vaug_ref[pl.ds(pl.multiple_of(j * tk, tk), tk), :]

    rows = lax.broadcasted_iota(jnp.int32, (tq, tk), 0)
    cols = lax.broadcasted_iota(jnp.int32, (tq, tk), 1)
    causal = cols <= rows
    for qh, acc_ref, m_ref in zip(qs, accs, ms):
        s = jnp.where(causal, scores(qh, qi), -jnp.inf)
        m = jnp.max(s, axis=1, keepdims=True)
        p = jnp.exp(s - m)
        acc_ref[...] = jnp.dot(p.astype(BF16), vblock(qi),
                               preferred_element_type=F32)
        m_ref[...] = m

    def body(j, carry):
        for qh, acc_ref, m_ref in zip(qs, accs, ms):
            s = scores(qh, j)
            m_old = m_ref[...]
            m_new = jnp.maximum(m_old, jnp.max(s, axis=1, keepdims=True))
            p = jnp.exp(s - m_new)
            alpha = jnp.exp(m_old - m_new)
            acc_ref[...] = alpha * acc_ref[...] + jnp.dot(
                p.astype(BF16), vblock(j), preferred_element_type=F32)
            m_ref[...] = m_new
        return carry

    lax.fori_loop(0, qi, body, 0)

    lam_p = lam_ref[...]
    lam = (jnp.exp(jnp.sum(lam_p[0:1] * lam_p[1:2], keepdims=True))
           - jnp.exp(jnp.sum(lam_p[2:3] * lam_p[3:4], keepdims=True))
           + lambda_init)
    a1 = acc1_ref[...]
    a2 = acc2_ref[...]
    o = (a1[:, :ATTN_V_DIM] / a1[:, ATTN_V_DIM:]
         - lam * (a2[:, :ATTN_V_DIM] / a2[:, ATTN_V_DIM:]))
    o = _rms(o, subln_ref[...], SUBLN_EPS) * (1.0 - lambda_init)
    o = o * _silu(gate_ref[...].astype(F32))
    o_ref[...] = o.astype(BF16)


def _attention(slopes, lam_p, subln, q, k, v, gate, *, batch, seq, lambda_init):
    m = q.shape[0]
    nq = seq // Q_TILE
    qspec = pl.BlockSpec((Q_TILE, ATTN_V_DIM), lambda b, h, i: (b * nq + i, h))
    kvspec = pl.BlockSpec((seq, ATTN_V_DIM), lambda b, h, i: (b, h))
    return pl.pallas_call(
        functools.partial(_attn_kernel, lambda_init=lambda_init),
        grid=(batch, ATTN_HEADS, nq),
        in_specs=[
            pl.BlockSpec(memory_space=pltpu.SMEM),
            _resident((4, ATTN_HEAD_DIM)),
            _resident((1, ATTN_V_DIM)),
            qspec, kvspec, kvspec, qspec,
        ],
        out_specs=qspec,
        out_shape=jax.ShapeDtypeStruct((m, ATTN_WIDTH), BF16),
        scratch_shapes=[
            pltpu.VMEM((seq, 2 * ATTN_V_DIM), BF16),
            pltpu.VMEM((Q_TILE, 2 * ATTN_V_DIM), F32),
            pltpu.VMEM((Q_TILE, 2 * ATTN_V_DIM), F32),
            pltpu.VMEM((Q_TILE, 1), F32),
            pltpu.VMEM((Q_TILE, 1), F32),
        ],
        compiler_params=_params(3),
        name="diff_attn",
    )(slopes, lam_p, subln, q, k, v, gate)


def _residual_ple(x, mix, p, wp_ref, wg_ref):
    x = x + mix
    e = jnp.dot(p.astype(BF16), wp_ref[...], preferred_element_type=F32)
    gate = jnp.dot(x.astype(BF16), wg_ref[...], preferred_element_type=F32)
    return x + e * jax.nn.sigmoid(gate)


def _attn_out_kernel(x_ref, y_ref, p_ref, wo_ref, wp_ref, wg_ref, o_ref):
    mix = jnp.dot(y_ref[...], wo_ref[...], preferred_element_type=F32)
    o_ref[...] = _residual_ple(x_ref[...], mix, p_ref[...], wp_ref, wg_ref)


def _attn_out(x, y, p, layer, wo, wp, wg):
    m = x.shape[0]
    row = pl.BlockSpec((ROW_BLOCK, D_MODEL), lambda i: (i, 0))
    prow = pl.BlockSpec((None, ROW_BLOCK, PLE_DIM), lambda i: (layer, i, 0))
    return pl.pallas_call(
        _attn_out_kernel,
        grid=(m // ROW_BLOCK,),
        in_specs=[row, row, prow, _resident((ATTN_WIDTH, D_MODEL)),
                  _resident((PLE_DIM, D_MODEL)), _resident((D_MODEL, D_MODEL))],
        out_specs=row,
        out_shape=jax.ShapeDtypeStruct((m, D_MODEL), F32),
        compiler_params=_params(1),
        name="attn_out",
    )(x, y, p, wo, wp, wg)


def _sgu_kernel(x_ref, p_ref, g_ref, win_ref, lng_ref, lnb_ref, ws_ref, bst_ref,
                wo_ref, wp_ref, wg_ref, fin_ref, o_ref, h_ref, v_ref, acc_ref,
                *, final):
    gd = SGU_GROUP_DIM
    x = x_ref[...]
    h_ref[...] = _rms(x, g_ref[...], NORM_EPS).astype(BF16)

    def proj(col0):
        return jnp.dot(h_ref[...], win_ref[:, col0:col0 + gd],
                       preferred_element_type=F32)

    vsum = jnp.zeros((ROW_BLOCK, 1), F32)
    vsq = jnp.zeros((ROW_BLOCK, 1), F32)
    for g in range(SGU_GROUPS):
        v = _gelu(proj(SGU_WIDTH + g * gd))
        v_ref[:, g * gd:(g + 1) * gd] = v
        vsum = vsum + jnp.sum(v, axis=1, keepdims=True)
        vsq = vsq + jnp.sum(v * v, axis=1, keepdims=True)
    mu = vsum * (1.0 / SGU_WIDTH)
    var = vsq * (1.0 / SGU_WIDTH) - mu * mu
    rstd = lax.rsqrt(var + LN_EPS)

    tril = (lax.broadcasted_iota(jnp.int32, (CHUNK, CHUNK), 0)
            >= lax.broadcasted_iota(jnp.int32, (CHUNK, CHUNK), 1))
    for g in range(SGU_GROUPS):
        sl = slice(g * gd, (g + 1) * gd)
        vn = ((v_ref[:, sl] - mu) * rstd * lng_ref[:, sl] + lnb_ref[:, sl]).astype(BF16)
        w = jnp.where(tril, ws_ref[g], 0.0).astype(BF16)
        bias = bst_ref[:, g:g + 1]
        mixed = jnp.concatenate(
            [jnp.dot(w, vn[c * CHUNK:(c + 1) * CHUNK], preferred_element_type=F32) + bias
             for c in range(ROW_BLOCK // CHUNK)], axis=0)
        u = _gelu(proj(g * gd))
        gate = proj(2 * SGU_WIDTH + g * gd)
        y = (u * mixed * _silu(gate)).astype(BF16)
        part = jnp.dot(y, wo_ref[sl, :], preferred_element_type=F32)
        if g == 0:
            acc_ref[...] = part
        else:
            acc_ref[...] += part

    out = _residual_ple(x, acc_ref[...], p_ref[...], wp_ref, wg_ref)
    if final:
        out = _rms(out, fin_ref[...], NORM_EPS)
    o_ref[...] = out


def _sgu_layer(x, p, layer, g, win, lng, lnb, ws, bst, wo, wp, wg, fin, *, final):
    m = x.shape[0]
    row = pl.BlockSpec((ROW_BLOCK, D_MODEL), lambda i: (i, 0))
    prow = pl.BlockSpec((None, ROW_BLOCK, PLE_DIM), lambda i: (layer, i, 0))
    return pl.pallas_call(
        functools.partial(_sgu_kernel, final=final),
        grid=(m // ROW_BLOCK,),
        in_specs=[row, prow, _resident((1, D_MODEL)),
                  _resident((D_MODEL, 3 * SGU_WIDTH)),
                  _resident((1, SGU_WIDTH)), _resident((1, SGU_WIDTH)),
                  _resident((SGU_GROUPS, CHUNK, CHUNK)),
                  _resident((CHUNK, SGU_GROUPS)),
                  _resident((SGU_WIDTH, D_MODEL)),
                  _resident((PLE_DIM, D_MODEL)), _resident((D_MODEL, D_MODEL)),
                  _resident((1, D_MODEL))],
        out_specs=row,
        out_shape=jax.ShapeDtypeStruct((m, D_MODEL), F32),
        scratch_shapes=[
            pltpu.VMEM((ROW_BLOCK, D_MODEL), BF16),
            pltpu.VMEM((ROW_BLOCK, SGU_WIDTH), F32),
            pltpu.VMEM((ROW_BLOCK, D_MODEL), F32),
        ],
        compiler_params=_params(1),
        name="sgu_layer",
    )(x, p, g, win, lng, lnb, ws, bst, wo, wp, wg, fin)


def kernel(x, p, attn_norm, attn_w_in, attn_lam_q1, attn_lam_k1, attn_lam_q2, attn_lam_k2, attn_subln, attn_w_out, sgu_norm, sgu_w_in, sgu_ln_g, sgu_ln_b, sgu_w_s, sgu_b_s, sgu_w_out, ple_proj, ple_gate, final_norm):
    batch, seq, d = x.shape
    m = batch * seq
    xr = x.reshape(m, d)
    pr = p.reshape(DEPTH, m, PLE_DIM)
    slopes = jnp.asarray(2.0 ** (-8.0 * jnp.arange(1, ATTN_HEADS + 1) / ATTN_HEADS), F32)
    fin = final_norm.reshape(1, d)
    for i in range(DEPTH):
        j = i // 2
        wp = ple_proj[i].astype(BF16)
        wg = ple_gate[i].astype(BF16)
        if i % 2 == 0:
            lambda_init = 0.8 - 0.6 * math.exp(-0.3 * i)
            q, k, v, gate = _attn_in(xr, attn_norm[j].reshape(1, d),
                                     attn_w_in[j].astype(BF16))
            lam_p = jnp.stack([attn_lam_q1[j], attn_lam_k1[j],
                               attn_lam_q2[j], attn_lam_k2[j]])
            y = _attention(slopes, lam_p, attn_subln[j].reshape(1, ATTN_V_DIM),
                           q, k, v, gate, batch=batch, seq=seq,
                           lambda_init=lambda_init)
            xr = _attn_out(xr, y, pr, i, attn_w_out[j].astype(BF16), wp, wg)
        else:
            xr = _sgu_layer(
                xr, pr, i, sgu_norm[j].reshape(1, d), sgu_w_in[j].astype(BF16),
                sgu_ln_g[j].reshape(1, SGU_WIDTH), sgu_ln_b[j].reshape(1, SGU_WIDTH),
                sgu_w_s[j], sgu_b_s[j].T, sgu_w_out[j].astype(BF16), wp, wg, fin,
                final=(i == DEPTH - 1))
    return xr.reshape(batch, seq, d)
```

```python
import functools
import math

import jax
import jax.numpy as jnp
from jax import lax
from jax.experimental import pallas as pl
from jax.experimental.pallas import tpu as pltpu

D_MODEL = 1024
DEPTH = 4
ATTN_HEADS = 8
ATTN_HEAD_DIM = 64
ATTN_V_DIM = 128
ATTN_WIDTH = 1024
SGU_WIDTH = 2048
SGU_GROUPS = 8
SGU_GROUP_DIM = 256
CHUNK = 128
PLE_DIM = 256
NORM_EPS = 1e-6
SUBLN_EPS = 1e-5
LN_EPS = 1e-5

F32 = jnp.float32
BF16 = jnp.bfloat16

ROW_BLOCK = 512
Q_TILE = 1024
K_TILE = 1024
VMEM_LIMIT = 56 * 1024 * 1024


def _rms(x, g, eps):
    return x * lax.rsqrt(jnp.mean(x * x, axis=-1, keepdims=True) + eps) * g


def _gelu(x):
    return 0.5 * x * (1.0 + lax.erf(x * (1.0 / math.sqrt(2.0))))


def _silu(x):
    return x * jax.nn.sigmoid(x)


def _resident(shape):
    zeros = (0,) * len(shape)
    return pl.BlockSpec(shape, lambda *_: zeros, pipeline_mode=pl.Buffered(1))


def _params(n_axes):
    return pltpu.CompilerParams(
        dimension_semantics=("arbitrary",) * n_axes,
        vmem_limit_bytes=VMEM_LIMIT)


def _attn_in_kernel(x_ref, g_ref, w_ref, q_ref, k_ref, v_ref, gate_ref):
    h = _rms(x_ref[...], g_ref[...], NORM_EPS).astype(BF16)
    outs = (q_ref, k_ref, v_ref, gate_ref)
    for idx, o_ref in enumerate(outs):
        z = jnp.dot(h, w_ref[:, idx * ATTN_WIDTH:(idx + 1) * ATTN_WIDTH],
                    preferred_element_type=F32)
        if idx == 0:
            z = z * (ATTN_HEAD_DIM ** -0.5)
        o_ref[...] = z.astype(BF16)


def _attn_in(x, g, w):
    m = x.shape[0]
    row = pl.BlockSpec((ROW_BLOCK, D_MODEL), lambda i: (i, 0))
    out = jax.ShapeDtypeStruct((m, ATTN_WIDTH), BF16)
    return pl.pallas_call(
        _attn_in_kernel,
        grid=(m // ROW_BLOCK,),
        in_specs=[row, _resident((1, D_MODEL)), _resident((D_MODEL, 4 * ATTN_WIDTH))],
        out_specs=[row] * 4,
        out_shape=[out] * 4,
        compiler_params=_params(1),
        name="attn_in",
    )(x, g, w)


ONES_ROWS = 16
POS_SPLIT = 64
Q_HALF = 512
LOOKAHEAD = 3


def _attn_kernel(slopes_ref, lam_ref, subln_ref, q_ref, k_ref, v_ref, gate_ref,
                 o_ref, kaug_ref, vt_ref, acc_ref, *, lambda_init):
    b = pl.program_id(0)
    hd = pl.program_id(1)
    qi = pl.program_id(2)
    tq, tk, dv = Q_TILE, K_TILE, ATTN_V_DIM
    seq = k_ref.shape[0]
    n_kv = seq // tk

    @pl.when((b == 0) & (hd == 0) & (qi == 0))
    def _():
        pos = lax.broadcasted_iota(jnp.int32, (seq, dv), 0)
        lane = lax.broadcasted_iota(jnp.int32, (seq, dv), 1)
        hi = (pos // POS_SPLIT) * POS_SPLIT
        lo = pos - hi
        val = jnp.where(lane == 0, hi, jnp.where(lane == 1, lo, 0))
        kaug_ref[:, dv:] = val.astype(F32).astype(BF16)

    @pl.when(qi == 0)
    def _():
        kaug_ref[:, :dv] = k_ref[...]
        for c in range(n_kv):
            vt_ref[c, :dv, :] = v_ref[c * tk:(c + 1) * tk, :].T
            vt_ref[c, dv:, :] = jnp.ones((ONES_ROWS, tk), BF16)

    slope = slopes_ref[hd]
    q = q_ref[...]
    lane = lax.broadcasted_iota(jnp.int32, q.shape, 1)
    zero = jnp.zeros_like(q)
    qpos = jnp.where(lane < 2, slope, 0.0).astype(BF16)
    qaug = (jnp.concatenate([jnp.where(lane < ATTN_HEAD_DIM, q, zero), qpos], axis=1),
            jnp.concatenate([jnp.where(lane >= ATTN_HEAD_DIM, q, zero), qpos], axis=1))
    chains = [(sm, c) for sm in range(2) for c in range(tq // Q_HALF)]

    def scores(sm, c, key0, n_keys):
        kb = kaug_ref[pl.ds(key0, n_keys), :]
        qa = qaug[sm][c * Q_HALF:(c + 1) * Q_HALF, :]
        return lax.dot_general(kb, qa, (((1,), (1,)), ((), ())),
                               preferred_element_type=F32)

    tri = (lax.broadcasted_iota(jnp.int32, (Q_HALF, Q_HALF), 0)
           <= lax.broadcasted_iota(jnp.int32, (Q_HALF, Q_HALF), 1))

    def kv_step(j, ms):
        diagonal = ms is None
        key0 = pl.multiple_of(j * tk, tk)
        n = len(chains)

        def n_keys(c):
            return (c + 1) * Q_HALF if diagonal else tk

        def sc(i):
            sm, c = chains[i]
            return scores(sm, c, key0, n_keys(c))

        pending = [sc(i) for i in range(min(LOOKAHEAD, n))]
        new_ms = []
        for i, (sm, c) in enumerate(chains):
            s = pending.pop(0)
            nk = n_keys(c)
            cols = slice(c * Q_HALF, (c + 1) * Q_HALF)
            if diagonal:
                tail = jnp.where(tri, s[nk - Q_HALF:], -jnp.inf)
                s = tail if nk == Q_HALF else jnp.concatenate([s[:nk - Q_HALF], tail], axis=0)
                m_new = jnp.max(s, axis=0, keepdims=True)
            else:
                m_new = jnp.maximum(ms[i], jnp.max(s, axis=0, keepdims=True))
            p = jnp.exp(s - m_new).astype(BF16)
            if i + LOOKAHEAD < n:
                pending.append(sc(i + LOOKAHEAD))
            pv = jnp.dot(vt_ref[j, :, :nk], p, preferred_element_type=F32)
            if diagonal:
                acc_ref[sm, :, cols] = pv
            else:
                acc_ref[sm, :, cols] = jnp.exp(ms[i] - m_new) * acc_ref[sm, :, cols] + pv
            new_ms.append(m_new)
        return tuple(new_ms)

    lax.fori_loop(0, qi, kv_step, kv_step(qi, None))

    lam_p = lam_ref[...]
    lam = (jnp.exp(jnp.sum(lam_p[0:1] * lam_p[1:2], keepdims=True))
           - jnp.exp(jnp.sum(lam_p[2:3] * lam_p[3:4], keepdims=True))
           + lambda_init)
    a1 = acc_ref[0]
    a2 = acc_ref[1]
    ot = a1[:dv] / a1[dv:dv + 1] - lam * (a2[:dv] / a2[dv:dv + 1])
    o = _rms(ot.T, subln_ref[...], SUBLN_EPS) * (1.0 - lambda_init)
    o = o * _silu(gate_ref[...].astype(F32))
    o_ref[...] = o.astype(BF16)


def _attention(slopes, lam_p, subln, q, k, v, gate, *, batch, seq, lambda_init):
    m = q.shape[0]
    nq = seq // Q_TILE
    qspec = pl.BlockSpec((Q_TILE, ATTN_V_DIM), lambda b, h, i: (b * nq + i, h))
    kvspec = pl.BlockSpec((seq, ATTN_V_DIM), lambda b, h, i: (b, h))
    return pl.pallas_call(
        functools.partial(_attn_kernel, lambda_init=lambda_init),
        grid=(batch, ATTN_HEADS, nq),
        in_specs=[
            pl.BlockSpec(memory_space=pltpu.SMEM),
            _resident((4, ATTN_HEAD_DIM)),
            _resident((1, ATTN_V_DIM)),
            qspec, kvspec, kvspec, qspec,
        ],
        out_specs=qspec,
        out_shape=jax.ShapeDtypeStruct((m, ATTN_WIDTH), BF16),
        scratch_shapes=[
            pltpu.VMEM((seq, 2 * ATTN_V_DIM), BF16),
            pltpu.VMEM((seq // K_TILE, ATTN_V_DIM + ONES_ROWS, K_TILE), BF16),
            pltpu.VMEM((2, ATTN_V_DIM + ONES_ROWS, Q_TILE), F32),
        ],
        compiler_params=_params(3),
        name="diff_attn",
    )(slopes, lam_p, subln, q, k, v, gate)


def _residual_ple(x, mix, p, wp_ref, wg_ref):
    x = x + mix
    e = jnp.dot(p.astype(BF16), wp_ref[...], preferred_element_type=F32)
    gate = jnp.dot(x.astype(BF16), wg_ref[...], preferred_element_type=F32)
    return x + e * jax.nn.sigmoid(gate)


def _attn_out_kernel(x_ref, y_ref, p_ref, wo_ref, wp_ref, wg_ref, o_ref):
    mix = jnp.dot(y_ref[...], wo_ref[...], preferred_element_type=F32)
    o_ref[...] = _residual_ple(x_ref[...], mix, p_ref[...], wp_ref, wg_ref)


def _attn_out(x, y, p, layer, wo, wp, wg):
    m = x.shape[0]
    row = pl.BlockSpec((ROW_BLOCK, D_MODEL), lambda i: (i, 0))
    prow = pl.BlockSpec((None, ROW_BLOCK, PLE_DIM), lambda i: (layer, i, 0))
    return pl.pallas_call(
        _attn_out_kernel,
        grid=(m // ROW_BLOCK,),
        in_specs=[row, row, prow, _resident((ATTN_WIDTH, D_MODEL)),
                  _resident((PLE_DIM, D_MODEL)), _resident((D_MODEL, D_MODEL))],
        out_specs=row,
        out_shape=jax.ShapeDtypeStruct((m, D_MODEL), F32),
        compiler_params=_params(1),
        name="attn_out",
    )(x, y, p, wo, wp, wg)


def _sgu_kernel(x_ref, p_ref, g_ref, win_ref, lng_ref, lnb_ref, ws_ref, bst_ref,
                wo_ref, wp_ref, wg_ref, fin_ref, o_ref, h_ref, v_ref, acc_ref,
                *, final):
    gd = SGU_GROUP_DIM
    x = x_ref[...]
    h_ref[...] = _rms(x, g_ref[...], NORM_EPS).astype(BF16)

    def proj(col0):
        return jnp.dot(h_ref[...], win_ref[:, col0:col0 + gd],
                       preferred_element_type=F32)

    vsum = jnp.zeros((ROW_BLOCK, 1), F32)
    vsq = jnp.zeros((ROW_BLOCK, 1), F32)
    for g in range(SGU_GROUPS):
        v = _gelu(proj(SGU_WIDTH + g * gd))
        v_ref[:, g * gd:(g + 1) * gd] = v
        vsum = vsum + jnp.sum(v, axis=1, keepdims=True)
        vsq = vsq + jnp.sum(v * v, axis=1, keepdims=True)
    mu = vsum * (1.0 / SGU_WIDTH)
    var = vsq * (1.0 / SGU_WIDTH) - mu * mu
    rstd = lax.rsqrt(var + LN_EPS)

    tril = (lax.broadcasted_iota(jnp.int32, (CHUNK, CHUNK), 0)
            >= lax.broadcasted_iota(jnp.int32, (CHUNK, CHUNK), 1))
    for g in range(SGU_GROUPS):
        sl = slice(g * gd, (g + 1) * gd)
        vn = ((v_ref[:, sl] - mu) * rstd * lng_ref[:, sl] + lnb_ref[:, sl]).astype(BF16)
        w = jnp.where(tril, ws_ref[g], 0.0).astype(BF16)
        bias = bst_ref[:, g:g + 1]
        mixed = jnp.concatenate(
            [jnp.dot(w, vn[c * CHUNK:(c + 1) * CHUNK], preferred_element_type=F32) + bias
             for c in range(ROW_BLOCK // CHUNK)], axis=0)
        u = _gelu(proj(g * gd))
        gate = proj(2 * SGU_WIDTH + g * gd)
        y = (u * mixed * _silu(gate)).astype(BF16)
        part = jnp.dot(y, wo_ref[sl, :], preferred_element_type=F32)
        if g == 0:
            acc_ref[...] = part
        else:
            acc_ref[...] += part

    out = _residual_ple(x, acc_ref[...], p_ref[...], wp_ref, wg_ref)
    if final:
        out = _rms(out, fin_ref[...], NORM_EPS)
    o_ref[...] = out


def _sgu_layer(x, p, layer, g, win, lng, lnb, ws, bst, wo, wp, wg, fin, *, final):
    m = x.shape[0]
    row = pl.BlockSpec((ROW_BLOCK, D_MODEL), lambda i: (i, 0))
    prow = pl.BlockSpec((None, ROW_BLOCK, PLE_DIM), lambda i: (layer, i, 0))
    return pl.pallas_call(
        functools.partial(_sgu_kernel, final=final),
        grid=(m // ROW_BLOCK,),
        in_specs=[row, prow, _resident((1, D_MODEL)),
                  _resident((D_MODEL, 3 * SGU_WIDTH)),
                  _resident((1, SGU_WIDTH)), _resident((1, SGU_WIDTH)),
                  _resident((SGU_GROUPS, CHUNK, CHUNK)),
                  _resident((CHUNK, SGU_GROUPS)),
                  _resident((SGU_WIDTH, D_MODEL)),
                  _resident((PLE_DIM, D_MODEL)), _resident((D_MODEL, D_MODEL)),
                  _resident((1, D_MODEL))],
        out_specs=row,
        out_shape=jax.ShapeDtypeStruct((m, D_MODEL), F32),
        scratch_shapes=[
            pltpu.VMEM((ROW_BLOCK, D_MODEL), BF16),
            pltpu.VMEM((ROW_BLOCK, SGU_WIDTH), F32),
            pltpu.VMEM((ROW_BLOCK, D_MODEL), F32),
        ],
        compiler_params=_params(1),
        name="sgu_layer",
    )(x, p, g, win, lng, lnb, ws, bst, wo, wp, wg, fin)


def kernel(x, p, attn_norm, attn_w_in, attn_lam_q1, attn_lam_k1, attn_lam_q2, attn_lam_k2, attn_subln, attn_w_out, sgu_norm, sgu_w_in, sgu_ln_g, sgu_ln_b, sgu_w_s, sgu_b_s, sgu_w_out, ple_proj, ple_gate, final_norm):
    batch, seq, d = x.shape
    m = batch * seq
    xr = x.reshape(m, d)
    pr = p.reshape(DEPTH, m, PLE_DIM)
    slopes = jnp.asarray(2.0 ** (-8.0 * jnp.arange(1, ATTN_HEADS + 1) / ATTN_HEADS), F32)
    fin = final_norm.reshape(1, d)
    for i in range(DEPTH):
        j = i // 2
        wp = ple_proj[i].astype(BF16)
        wg = ple_gate[i].astype(BF16)
        if i % 2 == 0:
            lambda_init = 0.8 - 0.6 * math.exp(-0.3 * i)
            q, k, v, gate = _attn_in(xr, attn_norm[j].reshape(1, d),
                                     attn_w_in[j].astype(BF16))
            lam_p = jnp.stack([attn_lam_q1[j], attn_lam_k1[j],
                               attn_lam_q2[j], attn_lam_k2[j]])
            y = _attention(slopes, lam_p, attn_subln[j].reshape(1, ATTN_V_DIM),
                           q, k, v, gate, batch=batch, seq=seq,
                           lambda_init=lambda_init)
            xr = _attn_out(xr, y, pr, i, attn_w_out[j].astype(BF16), wp, wg)
        else:
            xr = _sgu_layer(
                xr, pr, i, sgu_norm[j].reshape(1, d), sgu_w_in[j].astype(BF16),
                sgu_ln_g[j].reshape(1, SGU_WIDTH), sgu_ln_b[j].reshape(1, SGU_WIDTH),
                sgu_w_s[j], sgu_b_s[j].T, sgu_w_out[j].astype(BF16), wp, wg, fin,
                final=(i == DEPTH - 1))
    return xr.reshape(batch, seq, d)
```

```python
import functools
import math

import jax
import jax.numpy as jnp
from jax import lax
from jax.experimental import pallas as pl
from jax.experimental.pallas import tpu as pltpu

D_MODEL = 1024
DEPTH = 4
ATTN_HEADS = 8
ATTN_HEAD_DIM = 64
ATTN_V_DIM = 128
ATTN_WIDTH = 1024
SGU_WIDTH = 2048
SGU_GROUPS = 8
SGU_GROUP_DIM = 256
CHUNK = 128
PLE_DIM = 256
NORM_EPS = 1e-6
SUBLN_EPS = 1e-5
LN_EPS = 1e-5

F32 = jnp.float32
BF16 = jnp.bfloat16

ROW_BLOCK = 512
Q_TILE = 2048
K_TILE = 1024
VMEM_LIMIT = 56 * 1024 * 1024


def _rms(x, g, eps):
    return x * lax.rsqrt(jnp.mean(x * x, axis=-1, keepdims=True) + eps) * g


def _gelu(x):
    return 0.5 * x * (1.0 + lax.erf(x * (1.0 / math.sqrt(2.0))))


def _silu(x):
    return x * jax.nn.sigmoid(x)


def _resident(shape):
    zeros = (0,) * len(shape)
    return pl.BlockSpec(shape, lambda *_: zeros, pipeline_mode=pl.Buffered(1))


def _params(n_axes, flags=None):
    return pltpu.CompilerParams(
        dimension_semantics=("arbitrary",) * n_axes,
        vmem_limit_bytes=VMEM_LIMIT, flags=flags)


ONES_ROWS = 16
POS_SPLIT = 64
K_AUG = 2 * ATTN_V_DIM


def _attn_in_kernel(x_ref, g_ref, w_ref, q_ref, k_ref, vt_ref, gate_ref, *, seq):
    dv = ATTN_V_DIM
    h = _rms(x_ref[...], g_ref[...], NORM_EPS).astype(BF16)

    def proj(idx):
        return jnp.dot(h, w_ref[:, idx * ATTN_WIDTH:(idx + 1) * ATTN_WIDTH],
                       preferred_element_type=F32)

    v = proj(2)
    ones = jnp.ones((ONES_ROWS, ROW_BLOCK), BF16)
    for hd in range(ATTN_HEADS):
        vt_ref[hd, :dv, :] = v[:, hd * dv:(hd + 1) * dv].T.astype(BF16)
        vt_ref[hd, dv:, :] = ones

    row0 = lax.rem(pl.program_id(0) * ROW_BLOCK, seq)
    pos = row0 + lax.broadcasted_iota(jnp.int32, (ROW_BLOCK, dv), 0)
    lane = lax.broadcasted_iota(jnp.int32, (ROW_BLOCK, dv), 1)
    hi = (pos // POS_SPLIT) * POS_SPLIT
    pos_cols = jnp.where(lane == 0, hi, jnp.where(lane == 1, pos - hi, 0))
    pos_cols = pos_cols.astype(F32).astype(BF16)
    k = proj(1).astype(BF16)
    for hd in range(ATTN_HEADS):
        k_ref[:, hd * K_AUG:hd * K_AUG + dv] = k[:, hd * dv:(hd + 1) * dv]
        k_ref[:, hd * K_AUG + dv:(hd + 1) * K_AUG] = pos_cols

    q_ref[...] = (proj(0) * (ATTN_HEAD_DIM ** -0.5)).astype(BF16)
    gate_ref[...] = proj(3).astype(BF16)


def _attn_in(x, g, w, *, seq):
    m = x.shape[0]
    per_tile = K_TILE // ROW_BLOCK
    row = pl.BlockSpec((ROW_BLOCK, D_MODEL), lambda i: (i, 0))
    act = jax.ShapeDtypeStruct((m, ATTN_WIDTH), BF16)
    return pl.pallas_call(
        functools.partial(_attn_in_kernel, seq=seq),
        grid=(m // ROW_BLOCK,),
        in_specs=[row, _resident((1, D_MODEL)), _resident((D_MODEL, 4 * ATTN_WIDTH))],
        out_specs=[
            row,
            pl.BlockSpec((ROW_BLOCK, ATTN_HEADS * K_AUG), lambda i: (i, 0)),
            pl.BlockSpec((ATTN_HEADS, None, ATTN_V_DIM + ONES_ROWS, ROW_BLOCK),
                         lambda i: (0, i // per_tile, 0, i % per_tile)),
            row,
        ],
        out_shape=[
            act,
            jax.ShapeDtypeStruct((m, ATTN_HEADS * K_AUG), BF16),
            jax.ShapeDtypeStruct((ATTN_HEADS, m // K_TILE, ATTN_V_DIM + ONES_ROWS, K_TILE), BF16),
            act,
        ],
        compiler_params=_params(1),
        name="attn_in",
    )(x, g, w)


Q_CHAIN = 512
LOOKAHEAD = 4


def _attn_kernel(slopes_ref, lam_ref, q_ref, kaug_ref, vt_ref,
                 o_ref, qaug_ref, acc_ref, s_ref, *, lambda_init):
    hd = pl.program_id(1)
    qi = pl.program_id(2)
    tq, tk, dv = Q_TILE, K_TILE, ATTN_V_DIM

    slope = slopes_ref[hd]
    q = q_ref[...]
    lane = lax.broadcasted_iota(jnp.int32, q.shape, 1)
    zero = jnp.zeros_like(q)
    qpos = jnp.where(lane < 2, slope, 0.0).astype(BF16)
    qaug_ref[0, :, :dv] = jnp.where(lane < ATTN_HEAD_DIM, q, zero)
    qaug_ref[1, :, :dv] = jnp.where(lane >= ATTN_HEAD_DIM, q, zero)
    qaug_ref[0, :, dv:] = qpos
    qaug_ref[1, :, dv:] = qpos
    chains = [(sm, c) for sm in range(2) for c in range(tq // Q_CHAIN)]
    n = len(chains)

    def scores(sm, c, key0, n_keys):
        kb = kaug_ref[pl.ds(key0, n_keys), :]
        qa = qaug_ref[sm, c * Q_CHAIN:(c + 1) * Q_CHAIN, :]
        return lax.dot_general(kb, qa, (((1,), (1,)), ((), ())),
                               preferred_element_type=F32)

    tri = (lax.broadcasted_iota(jnp.int32, (Q_CHAIN, Q_CHAIN), 0)
           <= lax.broadcasted_iota(jnp.int32, (Q_CHAIN, Q_CHAIN), 1))

    def kv_step(j, ms, spec):
        key0 = pl.multiple_of(j * tk, tk)
        active = [i for i in range(n) if spec[i] is not None]

        def issue(a):
            sm, c = chains[active[a]]
            nk, masked = spec[active[a]]
            slot = a % (LOOKAHEAD + 1)
            s = scores(sm, c, key0, nk)
            if masked:
                tail = jnp.where(tri, s[nk - Q_CHAIN:], -jnp.inf)
                s = tail if nk == Q_CHAIN else jnp.concatenate([s[:nk - Q_CHAIN], tail], axis=0)
            s_ref[slot, :nk, :] = s
            return slot, jnp.max(s, axis=0, keepdims=True)

        pending = [issue(a) for a in range(min(LOOKAHEAD, len(active)))]
        new_ms = list(ms)
        for a, i in enumerate(active):
            sm, c = chains[i]
            nk, _ = spec[i]
            slot, m_tile = pending.pop(0)
            cols = slice(c * Q_CHAIN, (c + 1) * Q_CHAIN)
            m_new = jnp.maximum(ms[i], m_tile)
            p = jnp.exp(s_ref[slot, :nk, :] - m_new).astype(BF16)
            if a + LOOKAHEAD < len(active):
                pending.append(issue(a + LOOKAHEAD))
            pv = jnp.dot(vt_ref[j, :, :nk], p, preferred_element_type=F32)
            acc_ref[sm, :, cols] = jnp.exp(ms[i] - m_new) * acc_ref[sm, :, cols] + pv
            new_ms[i] = m_new
        return tuple(new_ms)

    acc_ref[...] = jnp.zeros(acc_ref.shape, F32)
    ms = (jnp.full((1, Q_CHAIN), -jnp.inf, F32),) * n
    tiles_per_q = tq // tk
    full = [(tk, False)] * n
    ms = lax.fori_loop(0, qi * tiles_per_q, lambda j, ms: kv_step(j, ms, full), ms)
    for d in range(tiles_per_q):
        spec = []
        for _, c in chains:
            visible = (c + 1) * Q_CHAIN - d * tk
            spec.append(None if visible <= 0 else (min(visible, tk), visible <= tk))
        ms = kv_step(qi * tiles_per_q + d, ms, spec)

    lam_p = lam_ref[...]
    lam = (jnp.exp(jnp.sum(lam_p[0:1] * lam_p[1:2], keepdims=True))
           - jnp.exp(jnp.sum(lam_p[2:3] * lam_p[3:4], keepdims=True))
           + lambda_init)
    a1 = acc_ref[0]
    a2 = acc_ref[1]
    ot = a1[:dv] / a1[dv:dv + 1] - lam * (a2[:dv] / a2[dv:dv + 1])
    o_ref[...] = ot.T.astype(BF16)


def _attention(slopes, lam_p, q, kaug, vt, *, batch, seq, lambda_init):
    m = q.shape[0]
    nq = seq // Q_TILE
    n_kv = seq // K_TILE
    qspec = pl.BlockSpec((Q_TILE, ATTN_V_DIM), lambda b, h, i: (b * nq + i, h))
    return pl.pallas_call(
        functools.partial(_attn_kernel, lambda_init=lambda_init),
        grid=(batch, ATTN_HEADS, nq),
        in_specs=[
            pl.BlockSpec(memory_space=pltpu.SMEM),
            _resident((4, ATTN_HEAD_DIM)),
            qspec,
            pl.BlockSpec((seq, K_AUG), lambda b, h, i: (b, h)),
            pl.BlockSpec((None, n_kv, ATTN_V_DIM + ONES_ROWS, K_TILE),
                         lambda b, h, i: (h, b, 0, 0)),
        ],
        out_specs=qspec,
        out_shape=jax.ShapeDtypeStruct((m, ATTN_WIDTH), BF16),
        scratch_shapes=[
            pltpu.VMEM((2, Q_TILE, K_AUG), BF16),
            pltpu.VMEM((2, ATTN_V_DIM + ONES_ROWS, Q_TILE), F32),
            pltpu.VMEM((LOOKAHEAD + 1, K_TILE, Q_CHAIN), F32),
        ],
        compiler_params=_params(3),
        name="diff_attn",
    )(slopes, lam_p, q, kaug, vt)


def _residual_ple(x, mix, p, wp_ref, wg_ref):
    x = x + mix
    e = jnp.dot(p.astype(BF16), wp_ref[...], preferred_element_type=F32)
    gate = jnp.dot(x.astype(BF16), wg_ref[...], preferred_element_type=F32)
    return x + e * jax.nn.sigmoid(gate)


def _attn_out_kernel(x_ref, y_ref, gate_ref, p_ref, subln_ref, wo_ref, wp_ref, wg_ref,
                     o_ref, *, lambda_init):
    dv = ATTN_V_DIM
    heads = []
    for hd in range(ATTN_HEADS):
        sl = slice(hd * dv, (hd + 1) * dv)
        o = _rms(y_ref[:, sl].astype(F32), subln_ref[...], SUBLN_EPS) * (1.0 - lambda_init)
        heads.append((o * _silu(gate_ref[:, sl].astype(F32))).astype(BF16))
    mix = jnp.dot(jnp.concatenate(heads, axis=1), wo_ref[...], preferred_element_type=F32)
    o_ref[...] = _residual_ple(x_ref[...], mix, p_ref[...], wp_ref, wg_ref)


def _attn_out(x, y, gate, p, layer, subln, wo, wp, wg, *, lambda_init):
    m = x.shape[0]
    row = pl.BlockSpec((ROW_BLOCK, D_MODEL), lambda i: (i, 0))
    prow = pl.BlockSpec((None, ROW_BLOCK, PLE_DIM), lambda i: (layer, i, 0))
    return pl.pallas_call(
        functools.partial(_attn_out_kernel, lambda_init=lambda_init),
        grid=(m // ROW_BLOCK,),
        in_specs=[row, row, row, prow, _resident((1, ATTN_V_DIM)),
                  _resident((ATTN_WIDTH, D_MODEL)),
                  _resident((PLE_DIM, D_MODEL)), _resident((D_MODEL, D_MODEL))],
        out_specs=row,
        out_shape=jax.ShapeDtypeStruct((m, D_MODEL), F32),
        compiler_params=_params(1),
        name="attn_out",
    )(x, y, gate, p, subln, wo, wp, wg)


def _sgu_kernel(x_ref, p_ref, g_ref, win_ref, lng_ref, lnb_ref, ws_ref, bst_ref,
                wo_ref, wp_ref, wg_ref, fin_ref, o_ref, h_ref, v_ref, acc_ref,
                *, final):
    gd = SGU_GROUP_DIM
    x = x_ref[...]
    h_ref[...] = _rms(x, g_ref[...], NORM_EPS).astype(BF16)

    def proj(col0):
        return jnp.dot(h_ref[...], win_ref[:, col0:col0 + gd],
                       preferred_element_type=F32)

    vsum = jnp.zeros((ROW_BLOCK, 1), F32)
    vsq = jnp.zeros((ROW_BLOCK, 1), F32)
    for g in range(SGU_GROUPS):
        v = _gelu(proj(SGU_WIDTH + g * gd))
        v_ref[:, g * gd:(g + 1) * gd] = v
        vsum = vsum + jnp.sum(v, axis=1, keepdims=True)
        vsq = vsq + jnp.sum(v * v, axis=1, keepdims=True)
    mu = vsum * (1.0 / SGU_WIDTH)
    var = vsq * (1.0 / SGU_WIDTH) - mu * mu
    rstd = lax.rsqrt(var + LN_EPS)

    tril = (lax.broadcasted_iota(jnp.int32, (CHUNK, CHUNK), 0)
            >= lax.broadcasted_iota(jnp.int32, (CHUNK, CHUNK), 1))
    for g in range(SGU_GROUPS):
        sl = slice(g * gd, (g + 1) * gd)
        vn = ((v_ref[:, sl] - mu) * rstd * lng_ref[:, sl] + lnb_ref[:, sl]).astype(BF16)
        w = jnp.where(tril, ws_ref[g], 0.0).astype(BF16)
        bias = bst_ref[:, g:g + 1]
        mixed = jnp.concatenate(
            [jnp.dot(w, vn[c * CHUNK:(c + 1) * CHUNK], preferred_element_type=F32) + bias
             for c in range(ROW_BLOCK // CHUNK)], axis=0)
        u = _gelu(proj(g * gd))
        gate = proj(2 * SGU_WIDTH + g * gd)
        y = (u * mixed * _silu(gate)).astype(BF16)
        part = jnp.dot(y, wo_ref[sl, :], preferred_element_type=F32)
        if g == 0:
            acc_ref[...] = part
        else:
            acc_ref[...] += part

    out = _residual_ple(x, acc_ref[...], p_ref[...], wp_ref, wg_ref)
    if final:
        out = _rms(out, fin_ref[...], NORM_EPS)
    o_ref[...] = out


def _sgu_layer(x, p, layer, g, win, lng, lnb, ws, bst, wo, wp, wg, fin, *, final):
    m = x.shape[0]
    row = pl.BlockSpec((ROW_BLOCK, D_MODEL), lambda i: (i, 0))
    prow = pl.BlockSpec((None, ROW_BLOCK, PLE_DIM), lambda i: (layer, i, 0))
    return pl.pallas_call(
        functools.partial(_sgu_kernel, final=final),
        grid=(m // ROW_BLOCK,),
        in_specs=[row, prow, _resident((1, D_MODEL)),
                  _resident((D_MODEL, 3 * SGU_WIDTH)),
                  _resident((1, SGU_WIDTH)), _resident((1, SGU_WIDTH)),
                  _resident((SGU_GROUPS, CHUNK, CHUNK)),
                  _resident((CHUNK, SGU_GROUPS)),
                  _resident((SGU_WIDTH, D_MODEL)),
                  _resident((PLE_DIM, D_MODEL)), _resident((D_MODEL, D_MODEL)),
                  _resident((1, D_MODEL))],
        out_specs=row,
        out_shape=jax.ShapeDtypeStruct((m, D_MODEL), F32),
        scratch_shapes=[
            pltpu.VMEM((ROW_BLOCK, D_MODEL), BF16),
            pltpu.VMEM((ROW_BLOCK, SGU_WIDTH), F32),
            pltpu.VMEM((ROW_BLOCK, D_MODEL), F32),
        ],
        compiler_params=_params(1),
        name="sgu_layer",
    )(x, p, g, win, lng, lnb, ws, bst, wo, wp, wg, fin)


def kernel(x, p, attn_norm, attn_w_in, attn_lam_q1, attn_lam_k1, attn_lam_q2, attn_lam_k2, attn_subln, attn_w_out, sgu_norm, sgu_w_in, sgu_ln_g, sgu_ln_b, sgu_w_s, sgu_b_s, sgu_w_out, ple_proj, ple_gate, final_norm):
    batch, seq, d = x.shape
    m = batch * seq
    xr = x.reshape(m, d)
    pr = p.reshape(DEPTH, m, PLE_DIM)
    slopes = jnp.asarray(2.0 ** (-8.0 * jnp.arange(1, ATTN_HEADS + 1) / ATTN_HEADS), F32)
    fin = final_norm.reshape(1, d)
    for i in range(DEPTH):
        j = i // 2
        wp = ple_proj[i].astype(BF16)
        wg = ple_gate[i].astype(BF16)
        if i % 2 == 0:
            lambda_init = 0.8 - 0.6 * math.exp(-0.3 * i)
            q, kaug, vt, gate = _attn_in(xr, attn_norm[j].reshape(1, d),
                                         attn_w_in[j].astype(BF16), seq=seq)
            lam_p = jnp.stack([attn_lam_q1[j], attn_lam_k1[j],
                               attn_lam_q2[j], attn_lam_k2[j]])
            y = _attention(slopes, lam_p, q, kaug, vt, batch=batch, seq=seq,
                           lambda_init=lambda_init)
            xr = _attn_out(xr, y, gate, pr, i, attn_subln[j].reshape(1, ATTN_V_DIM),
                           attn_w_out[j].astype(BF16), wp, wg, lambda_init=lambda_init)
        else:
            xr = _sgu_layer(
                xr, pr, i, sgu_norm[j].reshape(1, d), sgu_w_in[j].astype(BF16),
                sgu_ln_g[j].reshape(1, SGU_WIDTH), sgu_ln_b[j].reshape(1, SGU_WIDTH),
                sgu_w_s[j], sgu_b_s[j].T, sgu_w_out[j].astype(BF16), wp, wg, fin,
                final=(i == DEPTH - 1))
    return xr.reshape(batch, seq, d)
```

```python
import functools
import math

import jax
import jax.numpy as jnp
from jax import lax
from jax.experimental import pallas as pl
from jax.experimental.pallas import tpu as pltpu

D_MODEL = 1024
DEPTH = 4
ATTN_HEADS = 8
ATTN_HEAD_DIM = 64
ATTN_V_DIM = 128
ATTN_WIDTH = 1024
SGU_WIDTH = 2048
SGU_GROUPS = 8
SGU_GROUP_DIM = 256
CHUNK = 128
PLE_DIM = 256
NORM_EPS = 1e-6
SUBLN_EPS = 1e-5
LN_EPS = 1e-5

F32 = jnp.float32
BF16 = jnp.bfloat16

ROW_BLOCK = 1024
Q_TILE = 2048
K_TILE = 1024
VMEM_LIMIT = 56 * 1024 * 1024


def _rms(x, g, eps):
    return x * lax.rsqrt(jnp.mean(x * x, axis=-1, keepdims=True) + eps) * g


def _gelu(x):
    return 0.5 * x * (1.0 + lax.erf(x * (1.0 / math.sqrt(2.0))))


def _silu(x):
    return x * jax.nn.sigmoid(x)


def _resident(shape):
    zeros = (0,) * len(shape)
    return pl.BlockSpec(shape, lambda *_: zeros, pipeline_mode=pl.Buffered(1))


def _params(n_axes, flags=None):
    return pltpu.CompilerParams(
        dimension_semantics=("arbitrary",) * n_axes,
        vmem_limit_bytes=VMEM_LIMIT, flags=flags)


ONES_ROWS = 16
POS_SPLIT = 64
K_AUG = 2 * ATTN_V_DIM


def _attn_in_kernel(x_ref, g_ref, w_ref, q_ref, k_ref, vt_ref, gate_ref, *, seq):
    dv = ATTN_V_DIM
    h = _rms(x_ref[...], g_ref[...], NORM_EPS).astype(BF16)

    def proj(idx):
        return jnp.dot(h, w_ref[:, idx * ATTN_WIDTH:(idx + 1) * ATTN_WIDTH],
                       preferred_element_type=F32)

    v = proj(2)
    ones = jnp.ones((ONES_ROWS, ROW_BLOCK), BF16)
    for hd in range(ATTN_HEADS):
        vt_ref[hd, :dv, :] = v[:, hd * dv:(hd + 1) * dv].T.astype(BF16)
        vt_ref[hd, dv:, :] = ones

    row0 = lax.rem(pl.program_id(0) * ROW_BLOCK, seq)
    pos = row0 + lax.broadcasted_iota(jnp.int32, (ROW_BLOCK, dv), 0)
    lane = lax.broadcasted_iota(jnp.int32, (ROW_BLOCK, dv), 1)
    hi = (pos // POS_SPLIT) * POS_SPLIT
    pos_cols = jnp.where(lane == 0, hi, jnp.where(lane == 1, pos - hi, 0))
    pos_cols = pos_cols.astype(F32).astype(BF16)
    k = proj(1).astype(BF16)
    for hd in range(ATTN_HEADS):
        k_ref[:, hd * K_AUG:hd * K_AUG + dv] = k[:, hd * dv:(hd + 1) * dv]
        k_ref[:, hd * K_AUG + dv:(hd + 1) * K_AUG] = pos_cols

    q_ref[...] = (proj(0) * (ATTN_HEAD_DIM ** -0.5)).astype(BF16)
    gate_ref[...] = proj(3).astype(BF16)


def _attn_in(x, g, w, *, seq):
    m = x.shape[0]
    per_tile = K_TILE // ROW_BLOCK
    row = pl.BlockSpec((ROW_BLOCK, D_MODEL), lambda i: (i, 0))
    act = jax.ShapeDtypeStruct((m, ATTN_WIDTH), BF16)
    return pl.pallas_call(
        functools.partial(_attn_in_kernel, seq=seq),
        grid=(m // ROW_BLOCK,),
        in_specs=[row, _resident((1, D_MODEL)), _resident((D_MODEL, 4 * ATTN_WIDTH))],
        out_specs=[
            row,
            pl.BlockSpec((ROW_BLOCK, ATTN_HEADS * K_AUG), lambda i: (i, 0)),
            pl.BlockSpec((ATTN_HEADS, None, ATTN_V_DIM + ONES_ROWS, ROW_BLOCK),
                         lambda i: (0, i // per_tile, 0, i % per_tile)),
            row,
        ],
        out_shape=[
            act,
            jax.ShapeDtypeStruct((m, ATTN_HEADS * K_AUG), BF16),
            jax.ShapeDtypeStruct((ATTN_HEADS, m // K_TILE, ATTN_V_DIM + ONES_ROWS, K_TILE), BF16),
            act,
        ],
        compiler_params=_params(1),
        name="attn_in",
    )(x, g, w)


Q_CHAIN = 512
LOOKAHEAD = 6


def _attn_kernel(slopes_ref, lam_ref, q_ref, kaug_ref, vt_ref,
                 o_ref, qaug_ref, acc_ref, s_ref, *, lambda_init):
    hd = pl.program_id(1)
    qi = pl.program_id(2)
    tq, tk, dv = Q_TILE, K_TILE, ATTN_V_DIM

    slope = slopes_ref[hd]
    q = q_ref[...]
    lane = lax.broadcasted_iota(jnp.int32, q.shape, 1)
    zero = jnp.zeros_like(q)
    qpos = jnp.where(lane < 2, slope, 0.0).astype(BF16)
    qaug_ref[0, :, :dv] = jnp.where(lane < ATTN_HEAD_DIM, q, zero)
    qaug_ref[1, :, :dv] = jnp.where(lane >= ATTN_HEAD_DIM, q, zero)
    qaug_ref[0, :, dv:] = qpos
    qaug_ref[1, :, dv:] = qpos
    chains = [(sm, c) for sm in range(2) for c in range(tq // Q_CHAIN)]
    n = len(chains)

    def scores(sm, c, key0, n_keys):
        kb = kaug_ref[pl.ds(key0, n_keys), :]
        qa = qaug_ref[sm, c * Q_CHAIN:(c + 1) * Q_CHAIN, :]
        return lax.dot_general(kb, qa, (((1,), (1,)), ((), ())),
                               preferred_element_type=F32)

    tri = (lax.broadcasted_iota(jnp.int32, (Q_CHAIN, Q_CHAIN), 0)
           <= lax.broadcasted_iota(jnp.int32, (Q_CHAIN, Q_CHAIN), 1))

    def kv_step(j, ms, spec):
        key0 = pl.multiple_of(j * tk, tk)
        active = [i for i in range(n) if spec[i] is not None]

        def issue(a):
            sm, c = chains[active[a]]
            nk, masked = spec[active[a]]
            slot = a % (LOOKAHEAD + 1)
            s = scores(sm, c, key0, nk)
            if masked:
                tail = jnp.where(tri, s[nk - Q_CHAIN:], -jnp.inf)
                s = tail if nk == Q_CHAIN else jnp.concatenate([s[:nk - Q_CHAIN], tail], axis=0)
            s_ref[slot, :nk, :] = s
            return slot, jnp.max(s, axis=0, keepdims=True)

        pending = [issue(a) for a in range(min(LOOKAHEAD, len(active)))]
        new_ms = list(ms)
        for a, i in enumerate(active):
            sm, c = chains[i]
            nk, _ = spec[i]
            slot, m_tile = pending.pop(0)
            cols = slice(c * Q_CHAIN, (c + 1) * Q_CHAIN)
            m_new = m_tile if ms[i] is None else jnp.maximum(ms[i], m_tile)
            p = jnp.exp(s_ref[slot, :nk, :] - m_new).astype(BF16)
            if a + LOOKAHEAD < len(active):
                pending.append(issue(a + LOOKAHEAD))
            pv = jnp.dot(vt_ref[j, :, :nk], p, preferred_element_type=F32)
            if ms[i] is None:
                acc_ref[sm, :, cols] = pv
            else:
                acc_ref[sm, :, cols] = jnp.exp(ms[i] - m_new) * acc_ref[sm, :, cols] + pv
            new_ms[i] = m_new
        return tuple(new_ms)

    tiles_per_q = tq // tk
    ms = (None,) * n
    for d in range(tiles_per_q):
        spec = []
        for _, c in chains:
            visible = (c + 1) * Q_CHAIN - d * tk
            spec.append(None if visible <= 0 else (min(visible, tk), visible <= tk))
        ms = kv_step(qi * tiles_per_q + d, ms, spec)
    full = [(tk, False)] * n
    lax.fori_loop(0, qi * tiles_per_q, lambda j, ms: kv_step(j, ms, full), ms)

    lam_p = lam_ref[...]
    lam = (jnp.exp(jnp.sum(lam_p[0:1] * lam_p[1:2], keepdims=True))
           - jnp.exp(jnp.sum(lam_p[2:3] * lam_p[3:4], keepdims=True))
           + lambda_init)
    a1 = acc_ref[0]
    a2 = acc_ref[1]
    ot = a1[:dv] / a1[dv:dv + 1] - lam * (a2[:dv] / a2[dv:dv + 1])
    o_ref[...] = ot.T.astype(BF16)


def _attention(slopes, lam_p, q, kaug, vt, *, batch, seq, lambda_init):
    m = q.shape[0]
    nq = seq // Q_TILE
    n_kv = seq // K_TILE
    qspec = pl.BlockSpec((Q_TILE, ATTN_V_DIM), lambda b, h, i: (b * nq + i, h))
    return pl.pallas_call(
        functools.partial(_attn_kernel, lambda_init=lambda_init),
        grid=(batch, ATTN_HEADS, nq),
        in_specs=[
            pl.BlockSpec(memory_space=pltpu.SMEM),
            _resident((4, ATTN_HEAD_DIM)),
            qspec,
            pl.BlockSpec((seq, K_AUG), lambda b, h, i: (b, h)),
            pl.BlockSpec((None, n_kv, ATTN_V_DIM + ONES_ROWS, K_TILE),
                         lambda b, h, i: (h, b, 0, 0)),
        ],
        out_specs=qspec,
        out_shape=jax.ShapeDtypeStruct((m, ATTN_WIDTH), BF16),
        scratch_shapes=[
            pltpu.VMEM((2, Q_TILE, K_AUG), BF16),
            pltpu.VMEM((2, ATTN_V_DIM + ONES_ROWS, Q_TILE), F32),
            pltpu.VMEM((LOOKAHEAD + 1, K_TILE, Q_CHAIN), F32),
        ],
        compiler_params=_params(3),
        name="diff_attn",
    )(slopes, lam_p, q, kaug, vt)


def _residual_ple(x, mix, p, wp_ref, wg_ref):
    x = x + mix
    e = jnp.dot(p.astype(BF16), wp_ref[...], preferred_element_type=F32)
    gate = jnp.dot(x.astype(BF16), wg_ref[...], preferred_element_type=F32)
    return x + e * jax.nn.sigmoid(gate)


def _attn_out_kernel(x_ref, y_ref, gate_ref, p_ref, subln_ref, wo_ref, wp_ref, wg_ref,
                     o_ref, *, lambda_init):
    dv = ATTN_V_DIM
    heads = []
    for hd in range(ATTN_HEADS):
        sl = slice(hd * dv, (hd + 1) * dv)
        o = _rms(y_ref[:, sl].astype(F32), subln_ref[...], SUBLN_EPS) * (1.0 - lambda_init)
        heads.append((o * _silu(gate_ref[:, sl].astype(F32))).astype(BF16))
    mix = jnp.dot(jnp.concatenate(heads, axis=1), wo_ref[...], preferred_element_type=F32)
    o_ref[...] = _residual_ple(x_ref[...], mix, p_ref[...], wp_ref, wg_ref)


def _attn_out(x, y, gate, p, layer, subln, wo, wp, wg, *, lambda_init):
    m = x.shape[0]
    row = pl.BlockSpec((ROW_BLOCK, D_MODEL), lambda i: (i, 0))
    prow = pl.BlockSpec((None, ROW_BLOCK, PLE_DIM), lambda i: (layer, i, 0))
    return pl.pallas_call(
        functools.partial(_attn_out_kernel, lambda_init=lambda_init),
        grid=(m // ROW_BLOCK,),
        in_specs=[row, row, row, prow, _resident((1, ATTN_V_DIM)),
                  _resident((ATTN_WIDTH, D_MODEL)),
                  _resident((PLE_DIM, D_MODEL)), _resident((D_MODEL, D_MODEL))],
        out_specs=row,
        out_shape=jax.ShapeDtypeStruct((m, D_MODEL), F32),
        compiler_params=_params(1),
        name="attn_out",
    )(x, y, gate, p, subln, wo, wp, wg)


def _sgu_kernel(x_ref, p_ref, g_ref, win_ref, lng_ref, lnb_ref, ws_ref, bst_ref,
                wo_ref, wp_ref, wg_ref, fin_ref, o_ref, h_ref, v_ref, acc_ref,
                *, final):
    gd = SGU_GROUP_DIM
    x = x_ref[...]
    h_ref[...] = _rms(x, g_ref[...], NORM_EPS).astype(BF16)

    def proj(col0):
        return jnp.dot(h_ref[...], win_ref[:, col0:col0 + gd],
                       preferred_element_type=F32)

    vsum = jnp.zeros((ROW_BLOCK, 1), F32)
    vsq = jnp.zeros((ROW_BLOCK, 1), F32)
    for g in range(SGU_GROUPS):
        v = _gelu(proj(SGU_WIDTH + g * gd))
        v_ref[:, g * gd:(g + 1) * gd] = v
        vsum = vsum + jnp.sum(v, axis=1, keepdims=True)
        vsq = vsq + jnp.sum(v * v, axis=1, keepdims=True)
    mu = vsum * (1.0 / SGU_WIDTH)
    var = vsq * (1.0 / SGU_WIDTH) - mu * mu
    rstd = lax.rsqrt(var + LN_EPS)

    tril = (lax.broadcasted_iota(jnp.int32, (CHUNK, CHUNK), 0)
            >= lax.broadcasted_iota(jnp.int32, (CHUNK, CHUNK), 1))
    for g in range(SGU_GROUPS):
        sl = slice(g * gd, (g + 1) * gd)
        vn = ((v_ref[:, sl] - mu) * rstd * lng_ref[:, sl] + lnb_ref[:, sl]).astype(BF16)
        w = jnp.where(tril, ws_ref[g], 0.0).astype(BF16)
        bias = bst_ref[:, g:g + 1]
        mixed = jnp.concatenate(
            [jnp.dot(w, vn[c * CHUNK:(c + 1) * CHUNK], preferred_element_type=F32) + bias
             for c in range(ROW_BLOCK // CHUNK)], axis=0)
        u = _gelu(proj(g * gd))
        gate = proj(2 * SGU_WIDTH + g * gd)
        y = (u * mixed * _silu(gate)).astype(BF16)
        part = jnp.dot(y, wo_ref[sl, :], preferred_element_type=F32)
        if g == 0:
            acc_ref[...] = part
        else:
            acc_ref[...] += part

    out = _residual_ple(x, acc_ref[...], p_ref[...], wp_ref, wg_ref)
    if final:
        out = _rms(out, fin_ref[...], NORM_EPS)
    o_ref[...] = out


def _sgu_layer(x, p, layer, g, win, lng, lnb, ws, bst, wo, wp, wg, fin, *, final):
    m = x.shape[0]
    row = pl.BlockSpec((ROW_BLOCK, D_MODEL), lambda i: (i, 0))
    prow = pl.BlockSpec((None, ROW_BLOCK, PLE_DIM), lambda i: (layer, i, 0))
    return pl.pallas_call(
        functools.partial(_sgu_kernel, final=final),
        grid=(m // ROW_BLOCK,),
        in_specs=[row, prow, _resident((1, D_MODEL)),
                  _resident((D_MODEL, 3 * SGU_WIDTH)),
                  _resident((1, SGU_WIDTH)), _resident((1, SGU_WIDTH)),
                  _resident((SGU_GROUPS, CHUNK, CHUNK)),
                  _resident((CHUNK, SGU_GROUPS)),
                  _resident((SGU_WIDTH, D_MODEL)),
                  _resident((PLE_DIM, D_MODEL)), _resident((D_MODEL, D_MODEL)),
                  _resident((1, D_MODEL))],
        out_specs=row,
        out_shape=jax.ShapeDtypeStruct((m, D_MODEL), F32),
        scratch_shapes=[
            pltpu.VMEM((ROW_BLOCK, D_MODEL), BF16),
            pltpu.VMEM((ROW_BLOCK, SGU_WIDTH), F32),
            pltpu.VMEM((ROW_BLOCK, D_MODEL), F32),
        ],
        compiler_params=_params(1),
        name="sgu_layer",
    )(x, p, g, win, lng, lnb, ws, bst, wo, wp, wg, fin)


def kernel(x, p, attn_norm, attn_w_in, attn_lam_q1, attn_lam_k1, attn_lam_q2, attn_lam_k2, attn_subln, attn_w_out, sgu_norm, sgu_w_in, sgu_ln_g, sgu_ln_b, sgu_w_s, sgu_b_s, sgu_w_out, ple_proj, ple_gate, final_norm):
    batch, seq, d = x.shape
    m = batch * seq
    xr = x.reshape(m, d)
    pr = p.reshape(DEPTH, m, PLE_DIM)
    slopes = jnp.asarray(2.0 ** (-8.0 * jnp.arange(1, ATTN_HEADS + 1) / ATTN_HEADS), F32)
    fin = final_norm.reshape(1, d)
    for i in range(DEPTH):
        j = i // 2
        wp = ple_proj[i].astype(BF16)
        wg = ple_gate[i].astype(BF16)
        if i % 2 == 0:
            lambda_init = 0.8 - 0.6 * math.exp(-0.3 * i)
            q, kaug, vt, gate = _attn_in(xr, attn_norm[j].reshape(1, d),
                                         attn_w_in[j].astype(BF16), seq=seq)
            lam_p = jnp.stack([attn_lam_q1[j], attn_lam_k1[j],
                               attn_lam_q2[j], attn_lam_k2[j]])
            y = _attention(slopes, lam_p, q, kaug, vt, batch=batch, seq=seq,
                           lambda_init=lambda_init)
            xr = _attn_out(xr, y, gate, pr, i, attn_subln[j].reshape(1, ATTN_V_DIM),
                           attn_w_out[j].astype(BF16), wp, wg, lambda_init=lambda_init)
        else:
            xr = _sgu_layer(
                xr, pr, i, sgu_norm[j].reshape(1, d), sgu_w_in[j].astype(BF16),
                sgu_ln_g[j].reshape(1, SGU_WIDTH), sgu_ln_b[j].reshape(1, SGU_WIDTH),
                sgu_w_s[j], sgu_b_s[j].T, sgu_w_out[j].astype(BF16), wp, wg, fin,
                final=(i == DEPTH - 1))
    return xr.reshape(batch, seq, d)
```

```python
import functools
import math

import jax
import jax.numpy as jnp
from jax import lax
from jax.experimental import pallas as pl
from jax.experimental.pallas import tpu as pltpu

D_MODEL = 1024
DEPTH = 4
ATTN_HEADS = 8
ATTN_HEAD_DIM = 64
ATTN_V_DIM = 128
ATTN_WIDTH = 1024
SGU_WIDTH = 2048
SGU_GROUPS = 8
SGU_GROUP_DIM = 256
CHUNK = 128
PLE_DIM = 256
NORM_EPS = 1e-6
SUBLN_EPS = 1e-5
LN_EPS = 1e-5

F32 = jnp.float32
BF16 = jnp.bfloat16

ROW_BLOCK = 1024
Q_TILE = 2048
K_TILE = 2048
VMEM_LIMIT = 56 * 1024 * 1024


def _rms(x, g, eps):
    return x * lax.rsqrt(jnp.mean(x * x, axis=-1, keepdims=True) + eps) * g


def _gelu(x):
    return 0.5 * x * (1.0 + lax.erf(x * (1.0 / math.sqrt(2.0))))


def _sigmoid(x):
    return 0.5 * jnp.tanh(0.5 * x) + 0.5


def _silu(x):
    return x * _sigmoid(x)


def _resident(shape):
    zeros = (0,) * len(shape)
    return pl.BlockSpec(shape, lambda *_: zeros, pipeline_mode=pl.Buffered(1))


def _params(n_axes, flags=None):
    return pltpu.CompilerParams(
        dimension_semantics=("arbitrary",) * n_axes,
        vmem_limit_bytes=VMEM_LIMIT, flags=flags)


ONES_ROWS = 16
POS_SPLIT = 64
POS_TERMS = 3
LOG2E = math.log2(math.e)
K_AUG = 2 * ATTN_V_DIM


def _attn_in_kernel(x_ref, g_ref, w_ref, q_ref, k_ref, vt_ref, gate_ref, *, seq):
    dv = ATTN_V_DIM
    h = _rms(x_ref[...], g_ref[...], NORM_EPS).astype(BF16)

    def proj(idx):
        return jnp.dot(h, w_ref[:, idx * ATTN_WIDTH:(idx + 1) * ATTN_WIDTH],
                       preferred_element_type=F32)

    v = proj(2)
    ones = jnp.ones((ONES_ROWS, ROW_BLOCK), BF16)
    for hd in range(ATTN_HEADS):
        vt_ref[hd, :dv, :] = v[:, hd * dv:(hd + 1) * dv].T.astype(BF16)
        vt_ref[hd, dv:, :] = ones

    row0 = lax.rem(pl.program_id(0) * ROW_BLOCK, seq)
    pos = row0 + lax.broadcasted_iota(jnp.int32, (ROW_BLOCK, dv), 0)
    lane = lax.broadcasted_iota(jnp.int32, (ROW_BLOCK, dv), 1)
    hi = (pos // POS_SPLIT) * POS_SPLIT
    pos_cols = jnp.where(lane < 2 * POS_TERMS, jnp.where((lane & 1) == 0, hi, pos - hi), 0)
    pos_cols = pos_cols.astype(F32).astype(BF16)
    k = proj(1).astype(BF16)
    for hd in range(ATTN_HEADS):
        k_ref[:, hd * K_AUG:hd * K_AUG + dv] = k[:, hd * dv:(hd + 1) * dv]
        k_ref[:, hd * K_AUG + dv:(hd + 1) * K_AUG] = pos_cols

    q_ref[...] = (proj(0) * (ATTN_HEAD_DIM ** -0.5 * LOG2E)).astype(BF16)
    gate_ref[...] = proj(3).astype(BF16)


def _attn_in(x, g, w, *, seq):
    m = x.shape[0]
    per_tile = K_TILE // ROW_BLOCK
    row = pl.BlockSpec((ROW_BLOCK, D_MODEL), lambda i: (i, 0))
    act = jax.ShapeDtypeStruct((m, ATTN_WIDTH), BF16)
    return pl.pallas_call(
        functools.partial(_attn_in_kernel, seq=seq),
        grid=(m // ROW_BLOCK,),
        in_specs=[row, _resident((1, D_MODEL)), _resident((D_MODEL, 4 * ATTN_WIDTH))],
        out_specs=[
            row,
            pl.BlockSpec((ROW_BLOCK, ATTN_HEADS * K_AUG), lambda i: (i, 0)),
            pl.BlockSpec((ATTN_HEADS, None, ATTN_V_DIM + ONES_ROWS, ROW_BLOCK),
                         lambda i: (0, i // per_tile, 0, i % per_tile)),
            row,
        ],
        out_shape=[
            act,
            jax.ShapeDtypeStruct((m, ATTN_HEADS * K_AUG), BF16),
            jax.ShapeDtypeStruct((ATTN_HEADS, m // K_TILE, ATTN_V_DIM + ONES_ROWS, K_TILE), BF16),
            act,
        ],
        compiler_params=_params(1),
        name="attn_in",
    )(x, g, w)


Q_CHAIN = 512
LOOKAHEAD = 4


def _attn_kernel(slopes_ref, lam_ref, q_ref, kaug_ref, vt_ref,
                 o_ref, qaug_ref, acc_ref, s_ref, *, lambda_init):
    hd = pl.program_id(1)
    qi = pl.program_id(2)
    tq, tk, dv = Q_TILE, K_TILE, ATTN_V_DIM

    q = q_ref[...]
    lane = lax.broadcasted_iota(jnp.int32, q.shape, 1)
    zero = jnp.zeros_like(q)
    rest = jnp.full(q.shape, slopes_ref[hd], F32)
    qpos = jnp.zeros(q.shape, F32)
    for t in range(POS_TERMS):
        piece = rest.astype(BF16).astype(F32)
        rest = rest - piece
        qpos = jnp.where((lane >= 2 * t) & (lane < 2 * t + 2), piece, qpos)
    qpos = qpos.astype(BF16)
    qaug_ref[0, :, :dv] = jnp.where(lane < ATTN_HEAD_DIM, q, zero)
    qaug_ref[1, :, :dv] = jnp.where(lane >= ATTN_HEAD_DIM, q, zero)
    qaug_ref[0, :, dv:] = qpos
    qaug_ref[1, :, dv:] = qpos
    chains = [(sm, c) for sm in range(2) for c in range(tq // Q_CHAIN)]
    n = len(chains)

    def scores(sm, c, key0, n_keys):
        kb = kaug_ref[pl.ds(key0, n_keys), :]
        qa = qaug_ref[sm, c * Q_CHAIN:(c + 1) * Q_CHAIN, :]
        return lax.dot_general(kb, qa, (((1,), (1,)), ((), ())),
                               preferred_element_type=F32)

    tri = (lax.broadcasted_iota(jnp.int32, (Q_CHAIN, Q_CHAIN), 0)
           <= lax.broadcasted_iota(jnp.int32, (Q_CHAIN, Q_CHAIN), 1))

    def kv_step(j, ms, spec):
        key0 = pl.multiple_of(j * tk, tk)
        active = [i for i in range(n) if spec[i] is not None]

        def issue(a):
            sm, c = chains[active[a]]
            nk, masked = spec[active[a]]
            slot = a % (LOOKAHEAD + 1)
            s = scores(sm, c, key0, nk)
            if masked:
                tail = jnp.where(tri, s[nk - Q_CHAIN:], -jnp.inf)
                s = tail if nk == Q_CHAIN else jnp.concatenate([s[:nk - Q_CHAIN], tail], axis=0)
            s_ref[slot, :nk, :] = s
            return slot, jnp.max(s, axis=0, keepdims=True)

        pending = [issue(a) for a in range(min(LOOKAHEAD, len(active)))]
        new_ms = list(ms)
        for a, i in enumerate(active):
            sm, c = chains[i]
            nk, _ = spec[i]
            slot, m_tile = pending.pop(0)
            cols = slice(c * Q_CHAIN, (c + 1) * Q_CHAIN)
            m_new = m_tile if ms[i] is None else jnp.maximum(ms[i], m_tile)
            p = jnp.exp2(s_ref[slot, :nk, :] - m_new).astype(BF16)
            if a + LOOKAHEAD < len(active):
                pending.append(issue(a + LOOKAHEAD))
            pv = jnp.dot(vt_ref[j, :, :nk], p, preferred_element_type=F32)
            if ms[i] is None:
                acc_ref[sm, :, cols] = pv
            else:
                acc_ref[sm, :, cols] = jnp.exp2(ms[i] - m_new) * acc_ref[sm, :, cols] + pv
            new_ms[i] = m_new
        return tuple(new_ms)

    tiles_per_q = tq // tk
    ms = (None,) * n
    for d in range(tiles_per_q):
        spec = []
        for _, c in chains:
            visible = (c + 1) * Q_CHAIN - d * tk
            spec.append(None if visible <= 0 else (min(visible, tk), visible <= tk))
        ms = kv_step(qi * tiles_per_q + d, ms, spec)
    full = [(tk, False)] * n
    lax.fori_loop(0, qi * tiles_per_q, lambda j, ms: kv_step(j, ms, full), ms)

    lam_p = lam_ref[...]
    lam = (jnp.exp(jnp.sum(lam_p[0:1] * lam_p[1:2], keepdims=True))
           - jnp.exp(jnp.sum(lam_p[2:3] * lam_p[3:4], keepdims=True))
           + lambda_init)
    a1 = acc_ref[0]
    a2 = acc_ref[1]
    ot = a1[:dv] / a1[dv:dv + 1] - lam * (a2[:dv] / a2[dv:dv + 1])
    o_ref[...] = ot.T.astype(BF16)


def _attention(slopes, lam_p, q, kaug, vt, *, batch, seq, lambda_init):
    m = q.shape[0]
    nq = seq // Q_TILE
    n_kv = seq // K_TILE
    qspec = pl.BlockSpec((Q_TILE, ATTN_V_DIM), lambda b, h, i: (b * nq + i, h))
    return pl.pallas_call(
        functools.partial(_attn_kernel, lambda_init=lambda_init),
        grid=(batch, ATTN_HEADS, nq),
        in_specs=[
            pl.BlockSpec(memory_space=pltpu.SMEM),
            _resident((4, ATTN_HEAD_DIM)),
            qspec,
            pl.BlockSpec((seq, K_AUG), lambda b, h, i: (b, h)),
            pl.BlockSpec((None, n_kv, ATTN_V_DIM + ONES_ROWS, K_TILE),
                         lambda b, h, i: (h, b, 0, 0)),
        ],
        out_specs=qspec,
        out_shape=jax.ShapeDtypeStruct((m, ATTN_WIDTH), BF16),
        scratch_shapes=[
            pltpu.VMEM((2, Q_TILE, K_AUG), BF16),
            pltpu.VMEM((2, ATTN_V_DIM + ONES_ROWS, Q_TILE), F32),
            pltpu.VMEM((LOOKAHEAD + 1, K_TILE, Q_CHAIN), F32),
        ],
        compiler_params=_params(3),
        name="diff_attn",
    )(slopes, lam_p, q, kaug, vt)


def _residual_ple(x, mix, p, wp_ref, wg_ref):
    x = x + mix
    e = jnp.dot(p.astype(BF16), wp_ref[...], preferred_element_type=F32)
    gate = jnp.dot(x.astype(BF16), wg_ref[...], preferred_element_type=F32)
    return x + e * _sigmoid(gate)


def _attn_out_kernel(x_ref, y_ref, gate_ref, p_ref, subln_ref, wo_ref, wp_ref, wg_ref,
                     o_ref, *, lambda_init):
    dv = ATTN_V_DIM
    heads = []
    for hd in range(ATTN_HEADS):
        sl = slice(hd * dv, (hd + 1) * dv)
        o = _rms(y_ref[:, sl].astype(F32), subln_ref[...], SUBLN_EPS) * (1.0 - lambda_init)
        heads.append((o * _silu(gate_ref[:, sl].astype(F32))).astype(BF16))
    mix = jnp.dot(jnp.concatenate(heads, axis=1), wo_ref[...], preferred_element_type=F32)
    o_ref[...] = _residual_ple(x_ref[...], mix, p_ref[...], wp_ref, wg_ref)


def _attn_out(x, y, gate, p, layer, subln, wo, wp, wg, *, lambda_init):
    m = x.shape[0]
    row = pl.BlockSpec((ROW_BLOCK, D_MODEL), lambda i: (i, 0))
    prow = pl.BlockSpec((None, ROW_BLOCK, PLE_DIM), lambda i: (layer, i, 0))
    return pl.pallas_call(
        functools.partial(_attn_out_kernel, lambda_init=lambda_init),
        grid=(m // ROW_BLOCK,),
        in_specs=[row, row, row, prow, _resident((1, ATTN_V_DIM)),
                  _resident((ATTN_WIDTH, D_MODEL)),
                  _resident((PLE_DIM, D_MODEL)), _resident((D_MODEL, D_MODEL))],
        out_specs=row,
        out_shape=jax.ShapeDtypeStruct((m, D_MODEL), F32),
        compiler_params=_params(1),
        name="attn_out",
    )(x, y, gate, p, subln, wo, wp, wg)


def _sgu_kernel(x_ref, p_ref, g_ref, win_ref, lng_ref, lnb_ref, ws_ref, bst_ref,
                wo_ref, wp_ref, wg_ref, fin_ref, o_ref, h_ref, v_ref, acc_ref,
                *, final):
    gd = SGU_GROUP_DIM
    x = x_ref[...]
    h_ref[...] = _rms(x, g_ref[...], NORM_EPS).astype(BF16)

    def proj(col0):
        return jnp.dot(h_ref[...], win_ref[:, col0:col0 + gd],
                       preferred_element_type=F32)

    vsum = jnp.zeros((ROW_BLOCK, 1), F32)
    vsq = jnp.zeros((ROW_BLOCK, 1), F32)
    for g in range(SGU_GROUPS):
        v = _gelu(proj(SGU_WIDTH + g * gd))
        v_ref[:, g * gd:(g + 1) * gd] = v
        vsum = vsum + jnp.sum(v, axis=1, keepdims=True)
        vsq = vsq + jnp.sum(v * v, axis=1, keepdims=True)
    mu = vsum * (1.0 / SGU_WIDTH)
    var = vsq * (1.0 / SGU_WIDTH) - mu * mu
    rstd = lax.rsqrt(var + LN_EPS)

    tril = (lax.broadcasted_iota(jnp.int32, (CHUNK, CHUNK), 0)
            >= lax.broadcasted_iota(jnp.int32, (CHUNK, CHUNK), 1))
    for g in range(SGU_GROUPS):
        sl = slice(g * gd, (g + 1) * gd)
        vn = ((v_ref[:, sl] - mu) * rstd * lng_ref[:, sl] + lnb_ref[:, sl]).astype(BF16)
        w = jnp.where(tril, ws_ref[g], 0.0).astype(BF16)
        bias = bst_ref[:, g:g + 1]
        mixed = jnp.concatenate(
            [jnp.dot(w, vn[c * CHUNK:(c + 1) * CHUNK], preferred_element_type=F32) + bias
             for c in range(ROW_BLOCK // CHUNK)], axis=0)
        u = _gelu(proj(g * gd))
        gate = proj(2 * SGU_WIDTH + g * gd)
        y = (u * mixed * _silu(gate)).astype(BF16)
        part = jnp.dot(y, wo_ref[sl, :], preferred_element_type=F32)
        if g == 0:
            acc_ref[...] = part
        else:
            acc_ref[...] += part

    out = _residual_ple(x, acc_ref[...], p_ref[...], wp_ref, wg_ref)
    if final:
        out = _rms(out, fin_ref[...], NORM_EPS)
    o_ref[...] = out


def _sgu_layer(x, p, layer, g, win, lng, lnb, ws, bst, wo, wp, wg, fin, *, final):
    m = x.shape[0]
    row = pl.BlockSpec((ROW_BLOCK, D_MODEL), lambda i: (i, 0))
    prow = pl.BlockSpec((None, ROW_BLOCK, PLE_DIM), lambda i: (layer, i, 0))
    return pl.pallas_call(
        functools.partial(_sgu_kernel, final=final),
        grid=(m // ROW_BLOCK,),
        in_specs=[row, prow, _resident((1, D_MODEL)),
                  _resident((D_MODEL, 3 * SGU_WIDTH)),
                  _resident((1, SGU_WIDTH)), _resident((1, SGU_WIDTH)),
                  _resident((SGU_GROUPS, CHUNK, CHUNK)),
                  _resident((CHUNK, SGU_GROUPS)),
                  _resident((SGU_WIDTH, D_MODEL)),
                  _resident((PLE_DIM, D_MODEL)), _resident((D_MODEL, D_MODEL)),
                  _resident((1, D_MODEL))],
        out_specs=row,
        out_shape=jax.ShapeDtypeStruct((m, D_MODEL), F32),
        scratch_shapes=[
            pltpu.VMEM((ROW_BLOCK, D_MODEL), BF16),
            pltpu.VMEM((ROW_BLOCK, SGU_WIDTH), F32),
            pltpu.VMEM((ROW_BLOCK, D_MODEL), F32),
        ],
        compiler_params=_params(1),
        name="sgu_layer",
    )(x, p, g, win, lng, lnb, ws, bst, wo, wp, wg, fin)


def kernel(x, p, attn_norm, attn_w_in, attn_lam_q1, attn_lam_k1, attn_lam_q2, attn_lam_k2, attn_subln, attn_w_out, sgu_norm, sgu_w_in, sgu_ln_g, sgu_ln_b, sgu_w_s, sgu_b_s, sgu_w_out, ple_proj, ple_gate, final_norm):
    batch, seq, d = x.shape
    m = batch * seq
    xr = x.reshape(m, d)
    pr = p.reshape(DEPTH, m, PLE_DIM)
    slopes = jnp.asarray(2.0 ** (-8.0 * jnp.arange(1, ATTN_HEADS + 1) / ATTN_HEADS) * LOG2E, F32)
    fin = final_norm.reshape(1, d)
    for i in range(DEPTH):
        j = i // 2
        wp = ple_proj[i].astype(BF16)
        wg = ple_gate[i].astype(BF16)
        if i % 2 == 0:
            lambda_init = 0.8 - 0.6 * math.exp(-0.3 * i)
            q, kaug, vt, gate = _attn_in(xr, attn_norm[j].reshape(1, d),
                                         attn_w_in[j].astype(BF16), seq=seq)
            lam_p = jnp.stack([attn_lam_q1[j], attn_lam_k1[j],
                               attn_lam_q2[j], attn_lam_k2[j]])
            y = _attention(slopes, lam_p, q, kaug, vt, batch=batch, seq=seq,
                           lambda_init=lambda_init)
            xr = _attn_out(xr, y, gate, pr, i, attn_subln[j].reshape(1, ATTN_V_DIM),
                           attn_w_out[j].astype(BF16), wp, wg, lambda_init=lambda_init)
        else:
            xr = _sgu_layer(
                xr, pr, i, sgu_norm[j].reshape(1, d), sgu_w_in[j].astype(BF16),
                sgu_ln_g[j].reshape(1, SGU_WIDTH), sgu_ln_b[j].reshape(1, SGU_WIDTH),
                sgu_w_s[j], sgu_b_s[j].T, sgu_w_out[j].astype(BF16), wp, wg, fin,
                final=(i == DEPTH - 1))
    return xr.reshape(batch, seq, d)
```

```python
import functools
import math

import jax
import jax.numpy as jnp
from jax import lax
from jax.experimental import pallas as pl
from jax.experimental.pallas import tpu as pltpu

D_MODEL = 1024
DEPTH = 4
ATTN_HEADS = 8
ATTN_HEAD_DIM = 64
ATTN_V_DIM = 128
ATTN_WIDTH = 1024
SGU_WIDTH = 2048
SGU_GROUPS = 8
SGU_GROUP_DIM = 256
CHUNK = 128
PLE_DIM = 256
NORM_EPS = 1e-6
SUBLN_EPS = 1e-5
LN_EPS = 1e-5

F32 = jnp.float32
BF16 = jnp.bfloat16

ROW_BLOCK = 1024
Q_TILE = 2048
K_TILE = 2048
V7X_VMEM_BYTES = 64 * 1024 * 1024
VMEM_LIMIT = V7X_VMEM_BYTES * 7 // 8


def _rms(x, g, eps):
    return x * lax.rsqrt(jnp.mean(x * x, axis=-1, keepdims=True) + eps) * g


def _gelu(x):
    return 0.5 * x * (1.0 + lax.erf(x * (1.0 / math.sqrt(2.0))))


def _sigmoid(x):
    return 0.5 * jnp.tanh(0.5 * x) + 0.5


def _silu(x):
    return x * _sigmoid(x)


def _resident(shape):
    zeros = (0,) * len(shape)
    return pl.BlockSpec(shape, lambda *_: zeros, pipeline_mode=pl.Buffered(1))


def _params(n_axes):
    return pltpu.CompilerParams(
        dimension_semantics=("arbitrary",) * n_axes,
        vmem_limit_bytes=VMEM_LIMIT)


ONES_ROWS = 16
POS_SPLIT = 64
POS_TERMS = 3
LOG2E = math.log2(math.e)
K_AUG = 2 * ATTN_V_DIM


def _attn_in_kernel(x_ref, g_ref, w_ref, q_ref, k_ref, vt_ref, gate_ref, *, seq):
    dv = ATTN_V_DIM
    h = _rms(x_ref[...], g_ref[...], NORM_EPS).astype(BF16)

    def proj(idx):
        return jnp.dot(h, w_ref[:, idx * ATTN_WIDTH:(idx + 1) * ATTN_WIDTH],
                       preferred_element_type=F32)

    v = proj(2)
    ones = jnp.ones((ONES_ROWS, ROW_BLOCK), BF16)
    for hd in range(ATTN_HEADS):
        vt_ref[hd, :dv, :] = v[:, hd * dv:(hd + 1) * dv].T.astype(BF16)
        vt_ref[hd, dv:, :] = ones

    row0 = lax.rem(pl.program_id(0) * ROW_BLOCK, seq)
    pos = row0 + lax.broadcasted_iota(jnp.int32, (ROW_BLOCK, dv), 0)
    lane = lax.broadcasted_iota(jnp.int32, (ROW_BLOCK, dv), 1)
    hi = (pos // POS_SPLIT) * POS_SPLIT
    pos_cols = jnp.where(lane < 2 * POS_TERMS, jnp.where((lane & 1) == 0, hi, pos - hi), 0)
    pos_cols = pos_cols.astype(F32).astype(BF16)
    k = proj(1).astype(BF16)
    for hd in range(ATTN_HEADS):
        k_ref[:, hd * K_AUG:hd * K_AUG + dv] = k[:, hd * dv:(hd + 1) * dv]
        k_ref[:, hd * K_AUG + dv:(hd + 1) * K_AUG] = pos_cols

    q_ref[...] = (proj(0) * (ATTN_HEAD_DIM ** -0.5 * LOG2E)).astype(BF16)
    gate_ref[...] = proj(3).astype(BF16)


def _attn_in(x, g, w, *, seq):
    m = x.shape[0]
    per_tile = K_TILE // ROW_BLOCK
    row = pl.BlockSpec((ROW_BLOCK, D_MODEL), lambda i: (i, 0))
    act = jax.ShapeDtypeStruct((m, ATTN_WIDTH), BF16)
    return pl.pallas_call(
        functools.partial(_attn_in_kernel, seq=seq),
        grid=(m // ROW_BLOCK,),
        in_specs=[row, _resident((1, D_MODEL)), _resident((D_MODEL, 4 * ATTN_WIDTH))],
        out_specs=[
            row,
            pl.BlockSpec((ROW_BLOCK, ATTN_HEADS * K_AUG), lambda i: (i, 0)),
            pl.BlockSpec((ATTN_HEADS, None, ATTN_V_DIM + ONES_ROWS, ROW_BLOCK),
                         lambda i: (0, i // per_tile, 0, i % per_tile)),
            row,
        ],
        out_shape=[
            act,
            jax.ShapeDtypeStruct((m, ATTN_HEADS * K_AUG), BF16),
            jax.ShapeDtypeStruct((ATTN_HEADS, m // K_TILE, ATTN_V_DIM + ONES_ROWS, K_TILE), BF16),
            act,
        ],
        compiler_params=_params(1),
        name="attn_in",
    )(x, g, w)


Q_CHAIN = 512
LOOKAHEAD = 4


def _attn_kernel(slopes_ref, lam_ref, q_ref, kaug_ref, vt_ref,
                 o_ref, qaug_ref, acc_ref, s_ref, *, lambda_init):
    hd = pl.program_id(1)
    qi = pl.program_id(2)
    tq, tk, dv = Q_TILE, K_TILE, ATTN_V_DIM

    q = q_ref[...]
    lane = lax.broadcasted_iota(jnp.int32, q.shape, 1)
    zero = jnp.zeros_like(q)
    rest = jnp.full(q.shape, slopes_ref[hd], F32)
    qpos = jnp.zeros(q.shape, F32)
    for t in range(POS_TERMS):
        piece = rest.astype(BF16).astype(F32)
        rest = rest - piece
        qpos = jnp.where((lane >= 2 * t) & (lane < 2 * t + 2), piece, qpos)
    qpos = qpos.astype(BF16)
    qaug_ref[0, :, :dv] = jnp.where(lane < ATTN_HEAD_DIM, q, zero)
    qaug_ref[1, :, :dv] = jnp.where(lane >= ATTN_HEAD_DIM, q, zero)
    qaug_ref[0, :, dv:] = qpos
    qaug_ref[1, :, dv:] = qpos
    chains = [(sm, c) for sm in range(2) for c in range(tq // Q_CHAIN)]
    n = len(chains)

    def scores(sm, c, key0, n_keys):
        kb = kaug_ref[pl.ds(key0, n_keys), :]
        qa = qaug_ref[sm, c * Q_CHAIN:(c + 1) * Q_CHAIN, :]
        return lax.dot_general(kb, qa, (((1,), (1,)), ((), ())),
                               preferred_element_type=F32)

    tri = (lax.broadcasted_iota(jnp.int32, (Q_CHAIN, Q_CHAIN), 0)
           <= lax.broadcasted_iota(jnp.int32, (Q_CHAIN, Q_CHAIN), 1))

    def kv_step(j, ms, spec):
        key0 = pl.multiple_of(j * tk, tk)
        active = [i for i in range(n) if spec[i] is not None]

        def issue(a):
            sm, c = chains[active[a]]
            nk, masked = spec[active[a]]
            slot = a % (LOOKAHEAD + 1)
            s = scores(sm, c, key0, nk)
            if masked:
                tail = jnp.where(tri, s[nk - Q_CHAIN:], -jnp.inf)
                s = tail if nk == Q_CHAIN else jnp.concatenate([s[:nk - Q_CHAIN], tail], axis=0)
            s_ref[slot, :nk, :] = s
            return slot, jnp.max(s, axis=0, keepdims=True)

        pending = [issue(a) for a in range(min(LOOKAHEAD, len(active)))]
        new_ms = list(ms)
        for a, i in enumerate(active):
            sm, c = chains[i]
            nk, _ = spec[i]
            slot, m_tile = pending.pop(0)
            cols = slice(c * Q_CHAIN, (c + 1) * Q_CHAIN)
            m_new = m_tile if ms[i] is None else jnp.maximum(ms[i], m_tile)
            p = jnp.exp2(s_ref[slot, :nk, :] - m_new).astype(BF16)
            if a + LOOKAHEAD < len(active):
                pending.append(issue(a + LOOKAHEAD))
            pv = jnp.dot(vt_ref[j, :, :nk], p, preferred_element_type=F32)
            if ms[i] is None:
                acc_ref[sm, :, cols] = pv
            else:
                acc_ref[sm, :, cols] = jnp.exp2(ms[i] - m_new) * acc_ref[sm, :, cols] + pv
            new_ms[i] = m_new
        return tuple(new_ms)

    tiles_per_q = tq // tk
    ms = (None,) * n
    for d in range(tiles_per_q):
        spec = []
        for _, c in chains:
            visible = (c + 1) * Q_CHAIN - d * tk
            spec.append(None if visible <= 0 else (min(visible, tk), visible <= tk))
        ms = kv_step(qi * tiles_per_q + d, ms, spec)
    full = [(tk, False)] * n
    lax.fori_loop(0, qi * tiles_per_q, lambda j, ms: kv_step(j, ms, full), ms)

    lam_p = lam_ref[...]
    lam = (jnp.exp(jnp.sum(lam_p[0:1] * lam_p[1:2], keepdims=True))
           - jnp.exp(jnp.sum(lam_p[2:3] * lam_p[3:4], keepdims=True))
           + lambda_init)
    a1 = acc_ref[0]
    a2 = acc_ref[1]
    ot = a1[:dv] / a1[dv:dv + 1] - lam * (a2[:dv] / a2[dv:dv + 1])
    o_ref[...] = ot.T.astype(BF16)


def _attention(slopes, lam_p, q, kaug, vt, *, batch, seq, lambda_init):
    m = q.shape[0]
    nq = seq // Q_TILE
    n_kv = seq // K_TILE
    qspec = pl.BlockSpec((Q_TILE, ATTN_V_DIM), lambda b, h, i: (b * nq + i, h))
    return pl.pallas_call(
        functools.partial(_attn_kernel, lambda_init=lambda_init),
        grid=(batch, ATTN_HEADS, nq),
        in_specs=[
            pl.BlockSpec(memory_space=pltpu.SMEM),
            _resident((4, ATTN_HEAD_DIM)),
            qspec,
            pl.BlockSpec((seq, K_AUG), lambda b, h, i: (b, h)),
            pl.BlockSpec((None, n_kv, ATTN_V_DIM + ONES_ROWS, K_TILE),
                         lambda b, h, i: (h, b, 0, 0)),
        ],
        out_specs=qspec,
        out_shape=jax.ShapeDtypeStruct((m, ATTN_WIDTH), BF16),
        scratch_shapes=[
            pltpu.VMEM((2, Q_TILE, K_AUG), BF16),
            pltpu.VMEM((2, ATTN_V_DIM + ONES_ROWS, Q_TILE), F32),
            pltpu.VMEM((LOOKAHEAD + 1, K_TILE, Q_CHAIN), F32),
        ],
        compiler_params=_params(3),
        name="diff_attn",
    )(slopes, lam_p, q, kaug, vt)


def _residual_ple(x, mix, p, wp_ref, wg_ref):
    x = x + mix
    e = jnp.dot(p.astype(BF16), wp_ref[...], preferred_element_type=F32)
    gate = jnp.dot(x.astype(BF16), wg_ref[...], preferred_element_type=F32)
    return x + e * _sigmoid(gate)


def _attn_out_kernel(x_ref, y_ref, gate_ref, p_ref, subln_ref, wo_ref, wp_ref, wg_ref,
                     o_ref, *, lambda_init):
    dv = ATTN_V_DIM
    heads = []
    for hd in range(ATTN_HEADS):
        sl = slice(hd * dv, (hd + 1) * dv)
        o = _rms(y_ref[:, sl].astype(F32), subln_ref[...], SUBLN_EPS) * (1.0 - lambda_init)
        heads.append((o * _silu(gate_ref[:, sl].astype(F32))).astype(BF16))
    mix = jnp.dot(jnp.concatenate(heads, axis=1), wo_ref[...], preferred_element_type=F32)
    o_ref[...] = _residual_ple(x_ref[...], mix, p_ref[...], wp_ref, wg_ref)


def _attn_out(x, y, gate, p, layer, subln, wo, wp, wg, *, lambda_init):
    m = x.shape[0]
    row = pl.BlockSpec((ROW_BLOCK, D_MODEL), lambda i: (i, 0))
    prow = pl.BlockSpec((None, ROW_BLOCK, PLE_DIM), lambda i: (layer, i, 0))
    return pl.pallas_call(
        functools.partial(_attn_out_kernel, lambda_init=lambda_init),
        grid=(m // ROW_BLOCK,),
        in_specs=[row, row, row, prow, _resident((1, ATTN_V_DIM)),
                  _resident((ATTN_WIDTH, D_MODEL)),
                  _resident((PLE_DIM, D_MODEL)), _resident((D_MODEL, D_MODEL))],
        out_specs=row,
        out_shape=jax.ShapeDtypeStruct((m, D_MODEL), F32),
        compiler_params=_params(1),
        name="attn_out",
    )(x, y, gate, p, subln, wo, wp, wg)


def _sgu_kernel(x_ref, p_ref, g_ref, win_ref, lng_ref, lnb_ref, ws_ref, bst_ref,
                wo_ref, wp_ref, wg_ref, fin_ref, o_ref, h_ref, v_ref, acc_ref,
                *, final):
    gd = SGU_GROUP_DIM
    x = x_ref[...]
    h_ref[...] = _rms(x, g_ref[...], NORM_EPS).astype(BF16)

    def proj(col0):
        return jnp.dot(h_ref[...], win_ref[:, col0:col0 + gd],
                       preferred_element_type=F32)

    vsum = jnp.zeros((ROW_BLOCK, 1), F32)
    vsq = jnp.zeros((ROW_BLOCK, 1), F32)
    for g in range(SGU_GROUPS):
        v = _gelu(proj(SGU_WIDTH + g * gd))
        v_ref[:, g * gd:(g + 1) * gd] = v
        vsum = vsum + jnp.sum(v, axis=1, keepdims=True)
        vsq = vsq + jnp.sum(v * v, axis=1, keepdims=True)
    mu = vsum * (1.0 / SGU_WIDTH)
    var = vsq * (1.0 / SGU_WIDTH) - mu * mu
    rstd = lax.rsqrt(var + LN_EPS)

    tril = (lax.broadcasted_iota(jnp.int32, (CHUNK, CHUNK), 0)
            >= lax.broadcasted_iota(jnp.int32, (CHUNK, CHUNK), 1))
    for g in range(SGU_GROUPS):
        sl = slice(g * gd, (g + 1) * gd)
        vn = ((v_ref[:, sl] - mu) * rstd * lng_ref[:, sl] + lnb_ref[:, sl]).astype(BF16)
        w = jnp.where(tril, ws_ref[g], 0.0).astype(BF16)
        bias = bst_ref[:, g:g + 1]
        mixed = jnp.concatenate(
            [jnp.dot(w, vn[c * CHUNK:(c + 1) * CHUNK], preferred_element_type=F32) + bias
             for c in range(ROW_BLOCK // CHUNK)], axis=0)
        u = _gelu(proj(g * gd))
        gate = proj(2 * SGU_WIDTH + g * gd)
        y = (u * mixed * _silu(gate)).astype(BF16)
        part = jnp.dot(y, wo_ref[sl, :], preferred_element_type=F32)
        if g == 0:
            acc_ref[...] = part
        else:
            acc_ref[...] += part

    out = _residual_ple(x, acc_ref[...], p_ref[...], wp_ref, wg_ref)
    if final:
        out = _rms(out, fin_ref[...], NORM_EPS)
    o_ref[...] = out


def _sgu_layer(x, p, layer, g, win, lng, lnb, ws, bst, wo, wp, wg, fin, *, final):
    m = x.shape[0]
    row = pl.BlockSpec((ROW_BLOCK, D_MODEL), lambda i: (i, 0))
    prow = pl.BlockSpec((None, ROW_BLOCK, PLE_DIM), lambda i: (layer, i, 0))
    return pl.pallas_call(
        functools.partial(_sgu_kernel, final=final),
        grid=(m // ROW_BLOCK,),
        in_specs=[row, prow, _resident((1, D_MODEL)),
                  _resident((D_MODEL, 3 * SGU_WIDTH)),
                  _resident((1, SGU_WIDTH)), _resident((1, SGU_WIDTH)),
                  _resident((SGU_GROUPS, CHUNK, CHUNK)),
                  _resident((CHUNK, SGU_GROUPS)),
                  _resident((SGU_WIDTH, D_MODEL)),
                  _resident((PLE_DIM, D_MODEL)), _resident((D_MODEL, D_MODEL)),
                  _resident((1, D_MODEL))],
        out_specs=row,
        out_shape=jax.ShapeDtypeStruct((m, D_MODEL), F32),
        scratch_shapes=[
            pltpu.VMEM((ROW_BLOCK, D_MODEL), BF16),
            pltpu.VMEM((ROW_BLOCK, SGU_WIDTH), F32),
            pltpu.VMEM((ROW_BLOCK, D_MODEL), F32),
        ],
        compiler_params=_params(1),
        name="sgu_layer",
    )(x, p, g, win, lng, lnb, ws, bst, wo, wp, wg, fin)


def kernel(x, p, attn_norm, attn_w_in, attn_lam_q1, attn_lam_k1, attn_lam_q2, attn_lam_k2, attn_subln, attn_w_out, sgu_norm, sgu_w_in, sgu_ln_g, sgu_ln_b, sgu_w_s, sgu_b_s, sgu_w_out, ple_proj, ple_gate, final_norm):
    batch, seq, d = x.shape
    m = batch * seq
    xr = x.reshape(m, d)
    pr = p.reshape(DEPTH, m, PLE_DIM)
    slopes = jnp.asarray(2.0 ** (-8.0 * jnp.arange(1, ATTN_HEADS + 1) / ATTN_HEADS) * LOG2E, F32)
    fin = final_norm.reshape(1, d)
    for i in range(DEPTH):
        j = i // 2
        wp = ple_proj[i].astype(BF16)
        wg = ple_gate[i].astype(BF16)
        if i % 2 == 0:
            lambda_init = 0.8 - 0.6 * math.exp(-0.3 * i)
            q, kaug, vt, gate = _attn_in(xr, attn_norm[j].reshape(1, d),
                                         attn_w_in[j].astype(BF16), seq=seq)
            lam_p = jnp.stack([attn_lam_q1[j], attn_lam_k1[j],
                               attn_lam_q2[j], attn_lam_k2[j]])
            y = _attention(slopes, lam_p, q, kaug, vt, batch=batch, seq=seq,
                           lambda_init=lambda_init)
            xr = _attn_out(xr, y, gate, pr, i, attn_subln[j].reshape(1, ATTN_V_DIM),
                           attn_w_out[j].astype(BF16), wp, wg, lambda_init=lambda_init)
        else:
            xr = _sgu_layer(
                xr, pr, i, sgu_norm[j].reshape(1, d), sgu_w_in[j].astype(BF16),
                sgu_ln_g[j].reshape(1, SGU_WIDTH), sgu_ln_b[j].reshape(1, SGU_WIDTH),
                sgu_w_s[j], sgu_b_s[j].T, sgu_w_out[j].astype(BF16), wp, wg, fin,
                final=(i == DEPTH - 1))
    return xr.reshape(batch, seq, d)
```
